```python
import jax, jax.numpy as jnp
from jax import lax
import numpy as np

D_MODEL = 2048
BATCH = 2
SEQ = 4096
DEPTH = 4
DEC_BATCH = 1
DEC_SEQ = 16384
PAST_LEN = 128

N_MIXERS = 3
GRID_W = 64
HEAD_DIM = 128
N_HEADS = D_MODEL // HEAD_DIM
N_KV_HEADS = N_HEADS // 4
GQA_GROUP = N_HEADS // N_KV_HEADS
Q_BLOCK = 128
ROPE_THETA = 10000.0
ROPE_AXIS_DIM = HEAD_DIM // 2
FNET_GROUPS = 4
FNET_GROUP_DIM = D_MODEL // FNET_GROUPS
CONF_KERNEL = 31
FFN_DIM = 4 * D_MODEL
FFN_KERNEL = 3
EPS = 1e-6
N_ATTN = (DEPTH + 2) // 3
N_FNET = (DEPTH + 1) // 3
N_CONV = DEPTH // 3
QKV_DIM = (N_HEADS + 2 * N_KV_HEADS) * HEAD_DIM

kernel_name = "hybrid_bidir_encoder_attn_fnet_conformer"


def rmsnorm(x, g):
    xf = x.astype(jnp.float32)
    y = xf * lax.rsqrt(jnp.mean(xf * xf, axis=-1, keepdims=True) + EPS)
    return (y * g.astype(jnp.float32)).astype(x.dtype)


def layernorm(x, g, b):
    xf = x.astype(jnp.float32)
    mu = jnp.mean(xf, axis=-1, keepdims=True)
    xc = xf - mu
    y = xc * lax.rsqrt(jnp.mean(xc * xc, axis=-1, keepdims=True) + EPS)
    return (y * g.astype(jnp.float32) + b.astype(jnp.float32)).astype(x.dtype)


def depthwise_conv(x, w, b):
    k, c = w.shape
    pad = (k - 1) // 2
    y = lax.conv_general_dilated(
        x, w[:, None, :].astype(x.dtype), window_strides=(1,),
        padding=[(pad, pad)], dimension_numbers=("NWC", "WIO", "NWC"),
        feature_group_count=c)
    return y + b.astype(x.dtype)


def axial_rope_tables(n):
    rows = n // GRID_W
    row = jnp.repeat(jnp.arange(rows, dtype=jnp.float32), GRID_W)
    col = jnp.tile(jnp.arange(GRID_W, dtype=jnp.float32), rows)
    inv_freq = ROPE_THETA ** (-jnp.arange(0, ROPE_AXIS_DIM, 2, dtype=jnp.float32) / ROPE_AXIS_DIM)
    ang_r = row[:, None] * inv_freq[None, :]
    ang_c = col[:, None] * inv_freq[None, :]
    return jnp.cos(ang_r), jnp.sin(ang_r), jnp.cos(ang_c), jnp.sin(ang_c)


def _rotate(x, cos, sin):
    x1, x2 = jnp.split(x, 2, axis=-1)
    c = cos[None, :, None, :].astype(x.dtype)
    s = sin[None, :, None, :].astype(x.dtype)
    return jnp.concatenate([x1 * c - x2 * s, x2 * c + x1 * s], axis=-1)


def apply_axial_rope(x, rope):
    cos_r, sin_r, cos_c, sin_c = rope
    xr, xc = jnp.split(x, 2, axis=-1)
    return jnp.concatenate([_rotate(xr, cos_r, sin_r), _rotate(xc, cos_c, sin_c)], axis=-1)


def attention_mixer(h, rope, w_qkv, q_gain, k_gain, w_o):
    b, n, _ = h.shape
    qkv = h @ w_qkv
    q, k, v = jnp.split(qkv, [N_HEADS * HEAD_DIM, (N_HEADS + N_KV_HEADS) * HEAD_DIM], axis=-1)
    q = q.reshape(b, n, N_HEADS, HEAD_DIM)
    k = k.reshape(b, n, N_KV_HEADS, HEAD_DIM)
    v = v.reshape(b, n, N_KV_HEADS, HEAD_DIM)
    q = apply_axial_rope(rmsnorm(q, q_gain), rope)
    k = apply_axial_rope(rmsnorm(k, k_gain), rope)
    qb = q.reshape(b, n // Q_BLOCK, Q_BLOCK, N_KV_HEADS, GQA_GROUP, HEAD_DIM).transpose(1, 0, 2, 3, 4, 5)
    scale = HEAD_DIM ** -0.5

    def one_block(qblk):
        s = jnp.einsum("bqkgd,bskd->bkgqs", qblk, k, preferred_element_type=jnp.float32) * scale
        p = jax.nn.softmax(s, axis=-1).astype(v.dtype)
        return jnp.einsum("bkgqs,bskd->bqkgd", p, v)

    o = lax.map(one_block, qb)
    o = o.transpose(1, 0, 2, 3, 4, 5).reshape(b, n, N_HEADS * HEAD_DIM)
    return o @ w_o


def fourier_mixer(h, w_out, b_out):
    b, n, d = h.shape
    hg = h.astype(jnp.float32).reshape(b, n, FNET_GROUPS, FNET_GROUP_DIM)
    f = jnp.fft.fftn(hg, axes=(1, 3), norm="ortho").real
    return f.reshape(b, n, d).astype(h.dtype) @ w_out + b_out


def conformer_conv(h, w_pw1, b_pw1, dw_w, dw_b, ln_g, ln_b, w_pw2, b_pw2):
    a, gate = jnp.split(h @ w_pw1 + b_pw1, 2, axis=-1)
    u = a * jax.nn.sigmoid(gate)
    u = depthwise_conv(u, dw_w, dw_b)
    u = jax.nn.silu(layernorm(u, ln_g, ln_b))
    return u @ w_pw2 + b_pw2


def conv_ffn(h, w_up, dw_w, dw_b, w_down):
    u = depthwise_conv(h @ w_up, dw_w, dw_b)
    g, val = jnp.split(u, 2, axis=-1)
    return (jax.nn.gelu(g, approximate=True) * val) @ w_down


def run_trunk(x, p):
    rope = axial_rope_tables(x.shape[1])
    for i in range(DEPTH):
        j = i // N_MIXERS
        kind = i % N_MIXERS
        h = rmsnorm(x, p["norm_mix_pre"][i])
        if kind == 0:
            m = attention_mixer(h, rope, p["attn_w_qkv"][j], p["attn_q_gain"][j],
                                p["attn_k_gain"][j], p["attn_w_o"][j])
        elif kind == 1:
            m = fourier_mixer(h, p["fnet_w_out"][j], p["fnet_b_out"][j])
        else:
            m = conformer_conv(h, p["conv_w_pw1"][j], p["conv_b_pw1"][j], p["conv_dw_w"][j],
                               p["conv_dw_b"][j], p["conv_ln_g"][j], p["conv_ln_b"][j],
                               p["conv_w_pw2"][j], p["conv_b_pw2"][j])
        x = x + rmsnorm(m, p["norm_mix_post"][i])
        h = rmsnorm(x, p["norm_ffn_pre"][i])
        f = conv_ffn(h, p["ffn_w_up"][i], p["ffn_dw_w"][i], p["ffn_dw_b"][i], p["ffn_w_down"][i])
        x = x + rmsnorm(f, p["norm_ffn_post"][i])
    return x


def setup_inputs(seed: int = 0) -> dict:
    key = jax.random.key(seed)
    ks = iter(jax.random.split(key, 32))

    def nrm(shape, scale):
        return jax.random.normal(next(ks), shape, jnp.float32) * scale

    def gain(shape):
        return 1.0 + nrm(shape, 0.02)

    D = D_MODEL
    return {
        "x_prompt": nrm((BATCH, SEQ, D), 1.0),
        "x_sample": nrm((DEC_BATCH, DEC_SEQ, D), 1.0),
        "norm_mix_pre": gain((DEPTH, D)),
        "norm_mix_post": gain((DEPTH, D)),
        "norm_ffn_pre": gain((DEPTH, D)),
        "norm_ffn_post": gain((DEPTH, D)),
        "attn_w_qkv": nrm((N_ATTN, D, QKV_DIM), D ** -0.5),
        "attn_q_gain": gain((N_ATTN, HEAD_DIM)),
        "attn_k_gain": gain((N_ATTN, HEAD_DIM)),
        "attn_w_o": nrm((N_ATTN, N_HEADS * HEAD_DIM, D), (N_HEADS * HEAD_DIM) ** -0.5),
        "fnet_w_out": nrm((N_FNET, D, D), D ** -0.5),
        "fnet_b_out": nrm((N_FNET, D), 0.01),
        "conv_w_pw1": nrm((N_CONV, D, 2 * D), D ** -0.5),
        "conv_b_pw1": nrm((N_CONV, 2 * D), 0.01),
        "conv_dw_w": nrm((N_CONV, CONF_KERNEL, D), CONF_KERNEL ** -0.5),
        "conv_dw_b": nrm((N_CONV, D), 0.01),
        "conv_ln_g": gain((N_CONV, D)),
        "conv_ln_b": nrm((N_CONV, D), 0.01),
        "conv_w_pw2": nrm((N_CONV, D, D), D ** -0.5),
        "conv_b_pw2": nrm((N_CONV, D), 0.01),
        "ffn_w_up": nrm((DEPTH, D, 2 * FFN_DIM), D ** -0.5),
        "ffn_dw_w": nrm((DEPTH, FFN_KERNEL, 2 * FFN_DIM), FFN_KERNEL ** -0.5),
        "ffn_dw_b": nrm((DEPTH, 2 * FFN_DIM), 0.01),
        "ffn_w_down": nrm((DEPTH, FFN_DIM, D), FFN_DIM ** -0.5),
    }


def reference(x_prompt, x_sample, norm_mix_pre, norm_mix_post, norm_ffn_pre, norm_ffn_post,
              attn_w_qkv, attn_q_gain, attn_k_gain, attn_w_o,
              fnet_w_out, fnet_b_out,
              conv_w_pw1, conv_b_pw1, conv_dw_w, conv_dw_b, conv_ln_g, conv_ln_b, conv_w_pw2, conv_b_pw2,
              ffn_w_up, ffn_dw_w, ffn_dw_b, ffn_w_down):
    params = {
        "norm_mix_pre": norm_mix_pre, "norm_mix_post": norm_mix_post,
        "norm_ffn_pre": norm_ffn_pre, "norm_ffn_post": norm_ffn_post,
        "attn_w_qkv": attn_w_qkv, "attn_q_gain": attn_q_gain,
        "attn_k_gain": attn_k_gain, "attn_w_o": attn_w_o,
        "fnet_w_out": fnet_w_out, "fnet_b_out": fnet_b_out,
        "conv_w_pw1": conv_w_pw1, "conv_b_pw1": conv_b_pw1,
        "conv_dw_w": conv_dw_w, "conv_dw_b": conv_dw_b,
        "conv_ln_g": conv_ln_g, "conv_ln_b": conv_ln_b,
        "conv_w_pw2": conv_w_pw2, "conv_b_pw2": conv_b_pw2,
        "ffn_w_up": ffn_w_up, "ffn_dw_w": ffn_dw_w, "ffn_dw_b": ffn_dw_b, "ffn_w_down": ffn_w_down,
    }
    y_prompt = run_trunk(x_prompt, params)
    y_sample = run_trunk(x_sample, params)
    return (y_prompt, y_sample)
```

```python
import functools
import math

import numpy as np
import jax
import jax.numpy as jnp
from jax import lax
from jax.experimental import pallas as pl
from jax.experimental.pallas import tpu as pltpu

F32 = jnp.float32
BF16 = jnp.bfloat16

D_MODEL = 2048
DEPTH = 4
N_MIXERS = 3
GRID_W = 64
HEAD_DIM = 128
N_HEADS = D_MODEL // HEAD_DIM
N_KV_HEADS = N_HEADS // 4
GQA_GROUP = N_HEADS // N_KV_HEADS
ROPE_THETA = 10000.0
ROPE_AXIS_DIM = HEAD_DIM // 2
ROPE_F = ROPE_AXIS_DIM // 2
FNET_GROUPS = 4
FNET_GROUP_DIM = D_MODEL // FNET_GROUPS
CONF_KERNEL = 31
CONF_PAD = (CONF_KERNEL - 1) // 2
FFN_DIM = 4 * D_MODEL
EPS = 1e-6

V7X_VMEM_BYTES = 64 * 1024 * 1024
V7X_VMEM_LIMIT = V7X_VMEM_BYTES - 8 * 1024 * 1024
LANES = 128
SUBLANES_F32 = 8
SUBLANES_BF16 = 16
HALO = SUBLANES_BF16

ROW_TILE = 512
FFN_CHUNK = 512
CONF_ROW_TILE = 256
CONF_LANE_CHUNK = 512
ATTN_TQ = 256
ATTN_TK = 512
DFT_INTERLEAVE = SUBLANES_F32


def _params(semantics):
    return pltpu.CompilerParams(dimension_semantics=semantics,
                                vmem_limit_bytes=V7X_VMEM_LIMIT)


def _resident(shape, index_map):
    return pl.BlockSpec(shape, index_map, pipeline_mode=pl.Buffered(1))


def _rms(x, g):
    return x * lax.rsqrt(jnp.mean(x * x, axis=-1, keepdims=True) + EPS) * g


def _sigmoid(x):
    return 1.0 / (1.0 + jnp.exp(-x))


def _gelu_tanh(x):
    return 0.5 * x * (1.0 + jnp.tanh(math.sqrt(2.0 / math.pi) * (x + 0.044715 * (x * x * x))))


def _dot(a, b):
    return jnp.dot(a, b, preferred_element_type=F32)


def _store_sandwich(m, x_ref, gp_ref, gn_ref, o_ref, h_ref):
    xn = x_ref[...] + _rms(m, gp_ref[...])
    o_ref[...] = xn
    if h_ref is not None:
        h_ref[...] = _rms(xn, gn_ref[...]).astype(BF16)


def _rmsnorm_kernel(x_ref, g_ref, o_ref):
    o_ref[...] = _rms(x_ref[...], g_ref[...]).astype(o_ref.dtype)


def rmsnorm_bf16(x, g):
    t, d = x.shape
    return pl.pallas_call(
        _rmsnorm_kernel,
        grid=(t // ROW_TILE,),
        in_specs=[pl.BlockSpec((ROW_TILE, d), lambda i: (i, 0)),
                  pl.BlockSpec((1, d), lambda i: (0, 0))],
        out_specs=pl.BlockSpec((ROW_TILE, d), lambda i: (i, 0)),
        out_shape=jax.ShapeDtypeStruct((t, d), BF16),
        compiler_params=_params(("parallel",)),
        name="rmsnorm",
    )(x, g.reshape(1, d))


def _proj_res_kernel(*refs, has_bias, has_next):
    a_ref, w_ref = refs[0], refs[1]
    pos = 2
    b_ref = None
    if has_bias:
        b_ref = refs[pos]
        pos += 1
    x_ref, gp_ref = refs[pos], refs[pos + 1]
    pos += 2
    gn_ref = None
    if has_next:
        gn_ref = refs[pos]
        pos += 1
    o_ref = refs[pos]
    h_ref = refs[pos + 1] if has_next else None
    m = _dot(a_ref[...].astype(BF16), w_ref[...])
    if has_bias:
        m = m + b_ref[...]
    _store_sandwich(m, x_ref, gp_ref, gn_ref, o_ref, h_ref)


def proj_residual(a, w, bias, x, g_post, g_next):
    t, k = a.shape
    d = w.shape[1]
    has_bias = bias is not None
    has_next = g_next is not None
    row = lambda i: (i, 0)
    fixed = lambda i: (0, 0)
    in_specs = [pl.BlockSpec((ROW_TILE, k), row), _resident((k, d), fixed)]
    args = [a, w]
    if has_bias:
        in_specs.append(pl.BlockSpec((1, d), fixed))
        args.append(bias.reshape(1, d))
    in_specs += [pl.BlockSpec((ROW_TILE, d), row), pl.BlockSpec((1, d), fixed)]
    args += [x, g_post.reshape(1, d)]
    out_specs = [pl.BlockSpec((ROW_TILE, d), row)]
    out_shape = [jax.ShapeDtypeStruct((t, d), F32)]
    if has_next:
        in_specs.append(pl.BlockSpec((1, d), fixed))
        args.append(g_next.reshape(1, d))
        out_specs.append(pl.BlockSpec((ROW_TILE, d), row))
        out_shape.append(jax.ShapeDtypeStruct((t, d), BF16))
    outs = pl.pallas_call(
        functools.partial(_proj_res_kernel, has_bias=has_bias, has_next=has_next),
        grid=(t // ROW_TILE,),
        in_specs=in_specs, out_specs=out_specs, out_shape=out_shape,
        compiler_params=_params(("parallel",)),
        name="proj_residual",
    )(*args)
    return (outs[0], outs[1]) if has_next else (outs[0], None)


def _halo_specs(tm, d, n_tiles):
    per = tm // HALO
    last = n_tiles * per - 1
    return [pl.BlockSpec((HALO, d), lambda i, *_: (jnp.maximum(i * per - 1, 0), 0)),
            pl.BlockSpec((tm, d), lambda i, *_: (i, 0)),
            pl.BlockSpec((HALO, d), lambda i, *_: (jnp.minimum((i + 1) * per, last), 0))]


def _seq_edges(i, tiles_per_seq):
    pos = i % tiles_per_seq
    return pos == 0, pos == tiles_per_seq - 1


def _ffn_kernel(*refs, tm, tiles_per_seq, n_chunks, has_next):
    (hp_ref, hm_ref, hn_ref, wg_ref, wv_ref, cg_ref, cv_ref, bg_ref, bv_ref,
     wd_ref, x_ref, gp_ref) = refs[:12]
    pos = 12
    gn_ref = None
    if has_next:
        gn_ref = refs[pos]
        pos += 1
    o_ref = refs[pos]
    pos += 1
    h_ref = None
    if has_next:
        h_ref = refs[pos]
        pos += 1
    hh_ref, acc_ref = refs[pos], refs[pos + 1]

    i = pl.program_id(0)
    c = pl.program_id(1)

    @pl.when(c == 0)
    def _():
        first, last = _seq_edges(i, tiles_per_seq)
        keep_prev = jnp.where(first, 0.0, 1.0).astype(BF16)
        keep_next = jnp.where(last, 0.0, 1.0).astype(BF16)
        hh_ref[0:HALO, :] = hp_ref[...] * keep_prev
        hh_ref[HALO:HALO + tm, :] = hm_ref[...]
        hh_ref[HALO + tm:, :] = hn_ref[...] * keep_next
        acc_ref[...] = jnp.zeros_like(acc_ref)

    a = hh_ref[...]

    def conv3(v, w, b):
        return (w[0:1, :] * v[HALO - 1:HALO - 1 + tm] + w[1:2, :] * v[HALO:HALO + tm]
                + w[2:3, :] * v[HALO + 1:HALO + 1 + tm] + b)

    ug = conv3(_dot(a, wg_ref[...]), cg_ref[...], bg_ref[...])
    uv = conv3(_dot(a, wv_ref[...]), cv_ref[...], bv_ref[...])
    act = (_gelu_tanh(ug) * uv).astype(BF16)
    acc_ref[...] += _dot(act, wd_ref[...])

    @pl.when(c == n_chunks - 1)
    def _():
        _store_sandwich(acc_ref[...], x_ref, gp_ref, gn_ref, o_ref, h_ref)


def conv_ffn_residual(h, w_up, dw_w, dw_b, w_down, x, g_post, g_next, seq_len):
    t, d = h.shape
    f = w_down.shape[0]
    tm, tc = ROW_TILE, FFN_CHUNK
    n_tiles, n_chunks = t // tm, f // tc
    has_next = g_next is not None
    row = lambda i, c: (i, 0)
    fixed = lambda i, c: (0, 0)
    in_specs = _halo_specs(tm, d, n_tiles) + [
        pl.BlockSpec((d, tc), lambda i, c: (0, c)),
        pl.BlockSpec((d, tc), lambda i, c: (0, c + n_chunks)),
        pl.BlockSpec((3, tc), lambda i, c: (0, c)),
        pl.BlockSpec((3, tc), lambda i, c: (0, c + n_chunks)),
        pl.BlockSpec((1, tc), lambda i, c: (0, c)),
        pl.BlockSpec((1, tc), lambda i, c: (0, c + n_chunks)),
        pl.BlockSpec((tc, d), lambda i, c: (c, 0)),
        pl.BlockSpec((tm, d), row),
        pl.BlockSpec((1, d), fixed),
    ]
    dw_b2 = dw_b.reshape(1, 2 * f)
    args = [h, h, h, w_up, w_up, dw_w, dw_w, dw_b2, dw_b2, w_down, x, g_post.reshape(1, d)]
    out_specs = [pl.BlockSpec((tm, d), row)]
    out_shape = [jax.ShapeDtypeStruct((t, d), F32)]
    if has_next:
        in_specs.append(pl.BlockSpec((1, d), fixed))
        args.append(g_next.reshape(1, d))
        out_specs.append(pl.BlockSpec((tm, d), row))
        out_shape.append(jax.ShapeDtypeStruct((t, d), BF16))
    outs = pl.pallas_call(
        functools.partial(_ffn_kernel, tm=tm, tiles_per_seq=seq_len // tm,
                          n_chunks=n_chunks, has_next=has_next),
        grid=(n_tiles, n_chunks),
        in_specs=in_specs, out_specs=out_specs, out_shape=out_shape,
        scratch_shapes=[pltpu.VMEM((tm + 2 * HALO, d), BF16), pltpu.VMEM((tm, d), F32)],
        compiler_params=_params(("parallel", "arbitrary")),
        name="conv_ffn",
    )(*args)
    return (outs[0], outs[1]) if has_next else (outs[0], None)


def _head_norm_rope(x, gain, cr, sr, cc, sc):
    y = x * lax.rsqrt(jnp.mean(x * x, axis=0, keepdims=True) + EPS) * gain
    f = ROPE_F
    r1, r2, c1, c2 = y[0:f], y[f:2 * f], y[2 * f:3 * f], y[3 * f:4 * f]
    return jnp.concatenate([r1 * cr - r2 * sr, r2 * cr + r1 * sr,
                            c1 * cc - c2 * sc, c2 * cc + c1 * sc], axis=0)


def _qkv_kernel(h_ref, wt_ref, gq_ref, gk_ref, cr_ref, sr_ref, cc_ref, sc_ref,
                q_ref, k_ref, v_ref, *, tn, q_scale):
    h = h_ref[...]
    rope = (cr_ref[...], sr_ref[...], cc_ref[...], sc_ref[...])
    grp = GQA_GROUP * HEAD_DIM
    nt_dims = (((1,), (1,)), ((), ()))

    def proj(row0):
        return lax.dot_general(wt_ref[row0:row0 + grp, :], h, nt_dims,
                               preferred_element_type=F32)

    gq = gq_ref[...]
    for kv in range(N_KV_HEADS):
        r = proj(kv * grp)
        for g in range(GQA_GROUP):
            y = _head_norm_rope(r[g * HEAD_DIM:(g + 1) * HEAD_DIM], gq, *rope) * q_scale
            y = y.astype(BF16)
            for j in range(tn // ATTN_TQ):
                col = (j * GQA_GROUP + g) * ATTN_TQ
                q_ref[kv * HEAD_DIM:(kv + 1) * HEAD_DIM, col:col + ATTN_TQ] = (
                    y[:, j * ATTN_TQ:(j + 1) * ATTN_TQ])
    gk = gk_ref[...]
    r = proj(N_HEADS * HEAD_DIM)
    for kv in range(N_KV_HEADS):
        y = _head_norm_rope(r[kv * HEAD_DIM:(kv + 1) * HEAD_DIM], gk, *rope)
        k_ref[:, kv * HEAD_DIM:(kv + 1) * HEAD_DIM] = y.T.astype(BF16)
    r = proj((N_HEADS + N_KV_HEADS) * HEAD_DIM)
    ones = jnp.ones((SUBLANES_BF16, tn), BF16)
    for kv in range(N_KV_HEADS):
        v_ref[kv, 0, 0:HEAD_DIM, :] = r[kv * HEAD_DIM:(kv + 1) * HEAD_DIM].astype(BF16)
        v_ref[kv, 0, HEAD_DIM:, :] = ones


def qkv_project(h, w_t, q_gain, k_gain, rope_t, seq_len):
    t, d = h.shape
    tn = ATTN_TK
    per_seq = seq_len // tn
    q_scale = HEAD_DIM ** -0.5 * math.log2(math.e)
    gq = jnp.broadcast_to(q_gain.reshape(HEAD_DIM, 1), (HEAD_DIM, tn))
    gk = jnp.broadcast_to(k_gain.reshape(HEAD_DIM, 1), (HEAD_DIM, tn))
    rope_spec = pl.BlockSpec((ROPE_F, tn), lambda i: (0, i % per_seq))
    gain_spec = pl.BlockSpec((HEAD_DIM, tn), lambda i: (0, 0))
    return pl.pallas_call(
        functools.partial(_qkv_kernel, tn=tn, q_scale=q_scale),
        grid=(t // tn,),
        in_specs=[pl.BlockSpec((tn, d), lambda i: (i, 0)),
                  _resident(w_t.shape, lambda i: (0, 0)),
                  gain_spec, gain_spec, rope_spec, rope_spec, rope_spec, rope_spec],
        out_specs=[pl.BlockSpec((N_KV_HEADS * HEAD_DIM, GQA_GROUP * tn), lambda i: (0, i)),
                   pl.BlockSpec((tn, N_KV_HEADS * HEAD_DIM), lambda i: (i, 0)),
                   pl.BlockSpec((N_KV_HEADS, 1, HEAD_DIM + SUBLANES_BF16, tn),
                                lambda i: (0, i, 0, 0))],
        out_shape=[jax.ShapeDtypeStruct((N_KV_HEADS * HEAD_DIM, GQA_GROUP * t), BF16),
                   jax.ShapeDtypeStruct((t, N_KV_HEADS * HEAD_DIM), BF16),
                   jax.ShapeDtypeStruct((N_KV_HEADS, t // tn, HEAD_DIM + SUBLANES_BF16, tn),
                                        BF16)],
        compiler_params=_params(("parallel",)),
        name="qkv_project",
    )(h, w_t, gq, gk, *rope_t)


def _attn_kernel(q_ref, k_ref, v_ref, o_ref, m_ref, acc_ref, *, n_kblocks):
    m_ref[...] = jnp.full_like(m_ref, -jnp.inf)
    acc_ref[...] = jnp.zeros_like(acc_ref)
    q = q_ref[...]

    def body(c, carry):
        start = pl.multiple_of(c * ATTN_TK, ATTN_TK)
        s = _dot(k_ref[pl.ds(start, ATTN_TK), :], q)
        m_old = m_ref[...]
        m_new = jnp.maximum(m_old, jnp.max(s, axis=0, keepdims=True))
        p = jnp.exp2(s - m_new).astype(BF16)
        acc_ref[...] = acc_ref[...] * jnp.exp2(m_old - m_new) + _dot(v_ref[0, c], p)
        m_ref[...] = m_new
        return carry

    lax.fori_loop(0, n_kblocks, body, 0)
    acc = acc_ref[...]
    out = acc[0:HEAD_DIM] / acc[HEAD_DIM:HEAD_DIM + 1]
    for g in range(GQA_GROUP):
        o_ref[:, g * HEAD_DIM:(g + 1) * HEAD_DIM] = (
            out[:, g * ATTN_TQ:(g + 1) * ATTN_TQ].T.astype(o_ref.dtype))


def flash_attention(q, k, v, batch, seq_len):
    t = batch * seq_len
    q_per_seq = seq_len // ATTN_TQ
    n_kblocks = seq_len // ATTN_TK
    lanes = GQA_GROUP * ATTN_TQ
    return pl.pallas_call(
        functools.partial(_attn_kernel, n_kblocks=n_kblocks),
        grid=(batch, N_KV_HEADS, q_per_seq),
        in_specs=[pl.BlockSpec((HEAD_DIM, lanes), lambda b, kv, qi: (kv, b * q_per_seq + qi)),
                  pl.BlockSpec((seq_len, HEAD_DIM), lambda b, kv, qi: (b, kv)),
                  pl.BlockSpec((1, n_kblocks, HEAD_DIM + SUBLANES_BF16, ATTN_TK),
                               lambda b, kv, qi: (kv, b, 0, 0))],
        out_specs=pl.BlockSpec((ATTN_TQ, GQA_GROUP * HEAD_DIM),
                               lambda b, kv, qi: (b * q_per_seq + qi, kv)),
        out_shape=jax.ShapeDtypeStruct((t, N_HEADS * HEAD_DIM), BF16),
        scratch_shapes=[pltpu.VMEM((1, lanes), F32),
                        pltpu.VMEM((HEAD_DIM + SUBLANES_BF16, lanes), F32)],
        compiler_params=_params(("parallel", "parallel", "parallel")),
        name="flash_attention",
    )(q, k, v)


def _rope_tables_t(n):
    rows = n // GRID_W
    row = jnp.repeat(jnp.arange(rows, dtype=F32), GRID_W)
    col = jnp.tile(jnp.arange(GRID_W, dtype=F32), rows)
    inv_freq = ROPE_THETA ** (-jnp.arange(0, ROPE_AXIS_DIM, 2, dtype=F32) / ROPE_AXIS_DIM)
    ang_r = inv_freq[:, None] * row[None, :]
    ang_c = inv_freq[:, None] * col[None, :]
    return jnp.cos(ang_r), jnp.sin(ang_r), jnp.cos(ang_c), jnp.sin(ang_c)


def _dft_split(n):
    n2 = max(1, int(round(math.sqrt(n) / 4)))
    while n % n2:
        n2 -= 1
    return n // n2, n2


def _dft_tables(n):
    n1, n2 = _dft_split(n)
    k1 = np.arange(n1)
    ang1 = 2.0 * np.pi * ((k1[:, None] * k1[None, :]) % n1) / n1
    m1 = np.concatenate([np.cos(ang1), -np.sin(ang1)], axis=0)
    t2 = np.arange(n2)
    ang_tw = 2.0 * np.pi * (t2[:, None] * k1[None, :]) / n
    ang2 = 2.0 * np.pi * ((t2[:, None] * t2[None, :]) % n2) / n2
    eye = np.eye(DFT_INTERLEAVE)
    kc = np.einsum("kt,ij->kijt", np.cos(ang2), eye).reshape(n2 * DFT_INTERLEAVE, -1)
    ks = np.einsum("kt,ij->kijt", np.sin(ang2), eye).reshape(n2 * DFT_INTERLEAVE, -1)
    k_re = np.concatenate([kc, ks], axis=1)
    k_im = np.concatenate([-ks, kc], axis=1)
    c = np.arange(FNET_GROUP_DIM)
    ang_c = 2.0 * np.pi * ((c[:, None] * c[None, :]) % FNET_GROUP_DIM) / FNET_GROUP_DIM
    as_bf16 = lambda a: jnp.asarray(a, F32).astype(BF16)
    tw_shape = (n2, n1, LANES)
    return dict(
        n1=n1, n2=n2, m1=as_bf16(m1), k_re=as_bf16(k_re), k_im=as_bf16(k_im),
        cc=as_bf16(np.cos(ang_c)), sc=as_bf16(np.sin(ang_c)),
        tw_cos=jnp.broadcast_to(jnp.asarray(np.cos(ang_tw), F32)[:, :, None], tw_shape),
        tw_sin=jnp.broadcast_to(jnp.asarray(np.sin(ang_tw), F32)[:, :, None], tw_shape))


def _dft1_kernel(x_ref, m1_ref, twc_ref, tws_ref, o_ref, *, n1):
    r = _dot(m1_ref[...], x_ref[...])
    twc, tws = twc_ref[0], tws_ref[0]
    for j in range(x_ref.shape[1] // LANES):
        sl = slice(j * LANES, (j + 1) * LANES)
        ar, ai = r[0:n1, sl], r[n1:, sl]
        o_ref[0, 0, :, sl] = (ar * twc + ai * tws).astype(o_ref.dtype)
        o_ref[0, 1, :, sl] = (ai * twc - ar * tws).astype(o_ref.dtype)


def _dft2_kernel(a_ref, kre_ref, kim_ref, cc_ref, sc_ref, o_ref, *, rows, n2, scale):
    a = a_ref[0].reshape(2 * rows, a_ref.shape[-1])
    zr = _dot(kre_ref[...], a).astype(BF16)
    zi = _dot(kim_ref[...], a).astype(BF16)
    cc, sc = cc_ref[...], sc_ref[...]
    for g in range(FNET_GROUPS):
        sl = slice(g * FNET_GROUP_DIM, (g + 1) * FNET_GROUP_DIM)
        y = (_dot(zr[:, sl], cc) + _dot(zi[:, sl], sc)) * scale
        o_ref[0, :, :, sl] = y.reshape(n2, DFT_INTERLEAVE, FNET_GROUP_DIM)


def fourier_real_2d(h, batch, seq_len, tabs):
    t, d = h.shape
    n1, n2 = tabs["n1"], tabs["n2"]
    rows = n2 * DFT_INTERLEAVE
    a2 = pl.pallas_call(
        functools.partial(_dft1_kernel, n1=n1),
        grid=(batch, n2),
        in_specs=[pl.BlockSpec((n1, d), lambda b, j: (b, j)),
                  _resident((2 * n1, n1), lambda b, j: (0, 0)),
                  pl.BlockSpec((1, n1, LANES), lambda b, j: (j, 0, 0)),
                  pl.BlockSpec((1, n1, LANES), lambda b, j: (j, 0, 0))],
        out_specs=pl.BlockSpec((1, 2, n1, d), lambda b, j: (b, 0, 0, j)),
        out_shape=jax.ShapeDtypeStruct((batch, 2, n1, n2 * d), BF16),
        compiler_params=_params(("parallel", "parallel")),
        name="dft_stage1",
    )(h.reshape(batch * n1, n2 * d), tabs["m1"], tabs["tw_cos"], tabs["tw_sin"])
    scale = 1.0 / math.sqrt(seq_len * FNET_GROUP_DIM)
    y = pl.pallas_call(
        functools.partial(_dft2_kernel, rows=rows, n2=n2, scale=scale),
        grid=(batch, n1 // DFT_INTERLEAVE),
        in_specs=[pl.BlockSpec((1, 2, rows, d), lambda b, j: (b, 0, j, 0)),
                  _resident((rows, 2 * rows), lambda b, j: (0, 0)),
                  _resident((rows, 2 * rows), lambda b, j: (0, 0)),
                  _resident((FNET_GROUP_DIM, FNET_GROUP_DIM), lambda b, j: (0, 0)),
                  _resident((FNET_GROUP_DIM, FNET_GROUP_DIM), lambda b, j: (0, 0))],
        out_specs=pl.BlockSpec((1, n2, DFT_INTERLEAVE, d), lambda b, j: (b, 0, j, 0)),
        out_shape=jax.ShapeDtypeStruct((batch, n2, n1, d), F32),
        compiler_params=_params(("parallel", "parallel")),
        name="dft_stage2",
    )(a2.reshape(batch, 2, n1 * n2, d), tabs["k_re"], tabs["k_im"], tabs["cc"], tabs["sc"])
    return y.reshape(t, d)


def _conf_kernel(hp_ref, hm_ref, hn_ref, w1_ref, b1_ref, dw_ref, db_ref, lg_ref, lb_ref,
                 o_ref, hh_ref, y_ref, *, tm, tiles_per_seq):
    d = o_ref.shape[1]
    i = pl.program_id(0)
    first, last = _seq_edges(i, tiles_per_seq)
    hh_ref[0:HALO, :] = hp_ref[...]
    hh_ref[HALO:HALO + tm, :] = hm_ref[...]
    hh_ref[HALO + tm:, :] = hn_ref[...]
    a = hh_ref[...]
    ext = tm + 2 * HALO
    rows = lax.broadcasted_iota(jnp.int32, (ext, 1), 0)
    lo = jnp.where(first, HALO, 0)
    hi = jnp.where(last, HALO + tm, ext)
    inside = (rows >= lo) & (rows < hi)
    span = tm + SUBLANES_F32 * (2 * HALO // SUBLANES_F32 - 1)
    for j in range(d // CONF_LANE_CHUNK):
        sl = slice(j * CONF_LANE_CHUNK, (j + 1) * CONF_LANE_CHUNK)
        sg = slice(d + j * CONF_LANE_CHUNK, d + (j + 1) * CONF_LANE_CHUNK)
        za = _dot(a, w1_ref[:, sl]) + b1_ref[:, sl]
        zg = _dot(a, w1_ref[:, sg]) + b1_ref[:, sg]
        u = jnp.where(inside, za * _sigmoid(zg), 0.0)
        shifted = [u[b:b + span] for b in range(SUBLANES_F32)]
        y = jnp.zeros((tm, CONF_LANE_CHUNK), F32) + db_ref[:, sl]
        for k in range(CONF_KERNEL):
            off = HALO - CONF_PAD + k
            base = (off // SUBLANES_F32) * SUBLANES_F32
            y = y + dw_ref[k:k + 1, sl] * shifted[off % SUBLANES_F32][base:base + tm]
        y_ref[:, sl] = y
    y = y_ref[...]
    mu = jnp.mean(y, axis=-1, keepdims=True)
    yc = y - mu
    yn = yc * lax.rsqrt(jnp.mean(yc * yc, axis=-1, keepdims=True) + EPS) * lg_ref[...] + lb_ref[...]
    o_ref[...] = (yn * _sigmoid(yn)).astype(o_ref.dtype)


def conformer_inner(h, w1, b1, dw_w, dw_b, ln_g, ln_b, seq_len):
    t, d = h.shape
    tm = CONF_ROW_TILE
    n_tiles = t // tm
    fixed = lambda i: (0, 0)
    return pl.pallas_call(
        functools.partial(_conf_kernel, tm=tm, tiles_per_seq=seq_len // tm),
        grid=(n_tiles,),
        in_specs=_halo_specs(tm, d, n_tiles) + [
            _resident((d, 2 * d), fixed), pl.BlockSpec((1, 2 * d), fixed),
            pl.BlockSpec((CONF_KERNEL, d), fixed), pl.BlockSpec((1, d), fixed),
            pl.BlockSpec((1, d), fixed), pl.BlockSpec((1, d), fixed)],
        out_specs=pl.BlockSpec((tm, d), lambda i: (i, 0)),
        out_shape=jax.ShapeDtypeStruct((t, d), BF16),
        scratch_shapes=[pltpu.VMEM((tm + 2 * HALO, d), BF16), pltpu.VMEM((tm, d), F32)],
        compiler_params=_params(("parallel",)),
        name="conformer_inner",
    )(h, h, h, w1, b1.reshape(1, 2 * d), dw_w, dw_b.reshape(1, d),
      ln_g.reshape(1, d), ln_b.reshape(1, d))


def _run_trunk(x, p):
    batch, seq_len, d = x.shape
    t = batch * seq_len
    xf = x.reshape(t, d)
    rope_t = _rope_tables_t(seq_len)
    tabs = _dft_tables(seq_len)
    h = rmsnorm_bf16(xf, p["norm_mix_pre"][0])
    for i in range(DEPTH):
        j, kind = i // N_MIXERS, i % N_MIXERS
        g_post, g_ffn = p["norm_mix_post"][i], p["norm_ffn_pre"][i]
        if kind == 0:
            q, k, v = qkv_project(h, p["attn_w_qkv_t"][j], p["attn_q_gain"][j],
                                  p["attn_k_gain"][j], rope_t, seq_len)
            o = flash_attention(q, k, v, batch, seq_len)
            xf, h = proj_residual(o, p["attn_w_o"][j], None, xf, g_post, g_ffn)
        elif kind == 1:
            y = fourier_real_2d(h, batch, seq_len, tabs)
            xf, h = proj_residual(y, p["fnet_w_out"][j], p["fnet_b_out"][j], xf, g_post, g_ffn)
        else:
            u = conformer_inner(h, p["conv_w_pw1"][j], p["conv_b_pw1"][j], p["conv_dw_w"][j],
                                p["conv_dw_b"][j], p["conv_ln_g"][j], p["conv_ln_b"][j], seq_len)
            xf, h = proj_residual(u, p["conv_w_pw2"][j], p["conv_b_pw2"][j], xf, g_post, g_ffn)
        g_next = p["norm_mix_pre"][i + 1] if i + 1 < DEPTH else None
        xf, h = conv_ffn_residual(h, p["ffn_w_up"][i], p["ffn_dw_w"][i], p["ffn_dw_b"][i],
                                  p["ffn_w_down"][i], xf, p["norm_ffn_post"][i], g_next, seq_len)
    return xf.reshape(batch, seq_len, d)


def kernel(x_prompt, x_sample, norm_mix_pre, norm_mix_post, norm_ffn_pre, norm_ffn_post, attn_w_qkv, attn_q_gain, attn_k_gain, attn_w_o, fnet_w_out, fnet_b_out, conv_w_pw1, conv_b_pw1, conv_dw_w, conv_dw_b, conv_ln_g, conv_ln_b, conv_w_pw2, conv_b_pw2, ffn_w_up, ffn_dw_w, ffn_dw_b, ffn_w_down):
    p = {
        "norm_mix_pre": norm_mix_pre, "norm_mix_post": norm_mix_post,
        "norm_ffn_pre": norm_ffn_pre, "norm_ffn_post": norm_ffn_post,
        "attn_w_qkv_t": jnp.swapaxes(attn_w_qkv, 1, 2).astype(BF16),
        "attn_q_gain": attn_q_gain, "attn_k_gain": attn_k_gain,
        "attn_w_o": attn_w_o.astype(BF16),
        "fnet_w_out": fnet_w_out.astype(BF16), "fnet_b_out": fnet_b_out,
        "conv_w_pw1": conv_w_pw1.astype(BF16), "conv_b_pw1": conv_b_pw1,
        "conv_dw_w": conv_dw_w, "conv_dw_b": conv_dw_b,
        "conv_ln_g": conv_ln_g, "conv_ln_b": conv_ln_b,
        "conv_w_pw2": conv_w_pw2.astype(BF16), "conv_b_pw2": conv_b_pw2,
        "ffn_w_up": ffn_w_up.astype(BF16), "ffn_dw_w": ffn_dw_w, "ffn_dw_b": ffn_dw_b,
        "ffn_w_down": ffn_w_down.astype(BF16),
    }
    return (_run_trunk(x_prompt, p), _run_trunk(x_sample, p))
```

```python
import functools
import math

import numpy as np
import jax
import jax.numpy as jnp
from jax import lax
from jax.experimental import pallas as pl
from jax.experimental.pallas import tpu as pltpu

F32 = jnp.float32
BF16 = jnp.bfloat16

D_MODEL = 2048
DEPTH = 4
N_MIXERS = 3
GRID_W = 64
HEAD_DIM = 128
N_HEADS = D_MODEL // HEAD_DIM
N_KV_HEADS = N_HEADS // 4
GQA_GROUP = N_HEADS // N_KV_HEADS
ROPE_THETA = 10000.0
ROPE_AXIS_DIM = HEAD_DIM // 2
ROPE_F = ROPE_AXIS_DIM // 2
FNET_GROUPS = 4
FNET_GROUP_DIM = D_MODEL // FNET_GROUPS
CONF_KERNEL = 31
CONF_PAD = (CONF_KERNEL - 1) // 2
FFN_DIM = 4 * D_MODEL
EPS = 1e-6

V7X_VMEM_BYTES = 64 * 1024 * 1024
V7X_VMEM_LIMIT = V7X_VMEM_BYTES - 8 * 1024 * 1024
LANES = 128
SUBLANES_F32 = 8
SUBLANES_BF16 = 16
HALO = SUBLANES_BF16

ROW_TILE = 512
FFN_CHUNK = 512
FFN_ROW_BLOCKS = 2
CONF_ROW_TILE = 256
CONF_LANE_CHUNK = 512
ATTN_TQ = 256
ATTN_TK = 512
ATTN_SLOTS = 4
ATTN_MAX_FIXED_SHIFT = 60.0
ATTN_BOUND_SLACK = 1.02
DFT_INTERLEAVE = SUBLANES_F32


def _params(semantics):
    return pltpu.CompilerParams(dimension_semantics=semantics,
                                vmem_limit_bytes=V7X_VMEM_LIMIT)


def _resident(shape, index_map):
    return pl.BlockSpec(shape, index_map, pipeline_mode=pl.Buffered(1))


def _rms(x, g):
    return x * lax.rsqrt(jnp.mean(x * x, axis=-1, keepdims=True) + EPS) * g


def _sigmoid(x):
    return 1.0 / (1.0 + jnp.exp(-x))


def _gelu_tanh(x):
    a = -2.0 * math.sqrt(2.0 / math.pi) * math.log2(math.e)
    return x / (1.0 + jnp.exp2(x * (a + (a * 0.044715) * (x * x))))


def _dot(a, b):
    return jnp.dot(a, b, preferred_element_type=F32)


def _store_sandwich(m, x_ref, gp_ref, gn_ref, o_ref, h_ref):
    xn = x_ref[...] + _rms(m, gp_ref[...])
    o_ref[...] = xn
    if h_ref is not None:
        h_ref[...] = _rms(xn, gn_ref[...]).astype(BF16)


def _rmsnorm_kernel(x_ref, g_ref, o_ref):
    o_ref[...] = _rms(x_ref[...], g_ref[...]).astype(o_ref.dtype)


def rmsnorm_bf16(x, g):
    t, d = x.shape
    return pl.pallas_call(
        _rmsnorm_kernel,
        grid=(t // ROW_TILE,),
        in_specs=[pl.BlockSpec((ROW_TILE, d), lambda i: (i, 0)),
                  pl.BlockSpec((1, d), lambda i: (0, 0))],
        out_specs=pl.BlockSpec((ROW_TILE, d), lambda i: (i, 0)),
        out_shape=jax.ShapeDtypeStruct((t, d), BF16),
        compiler_params=_params(("parallel",)),
        name="rmsnorm",
    )(x, g.reshape(1, d))


def _proj_res_kernel(*refs, has_bias, has_next):
    a_ref, w_ref = refs[0], refs[1]
    pos = 2
    b_ref = None
    if has_bias:
        b_ref = refs[pos]
        pos += 1
    x_ref, gp_ref = refs[pos], refs[pos + 1]
    pos += 2
    gn_ref = None
    if has_next:
        gn_ref = refs[pos]
        pos += 1
    o_ref = refs[pos]
    h_ref = refs[pos + 1] if has_next else None
    m = _dot(a_ref[...].astype(BF16), w_ref[...])
    if has_bias:
        m = m + b_ref[...]
    _store_sandwich(m, x_ref, gp_ref, gn_ref, o_ref, h_ref)


def proj_residual(a, w, bias, x, g_post, g_next):
    t, k = a.shape
    d = w.shape[1]
    has_bias = bias is not None
    has_next = g_next is not None
    row = lambda i: (i, 0)
    fixed = lambda i: (0, 0)
    in_specs = [pl.BlockSpec((ROW_TILE, k), row), _resident((k, d), fixed)]
    args = [a, w]
    if has_bias:
        in_specs.append(pl.BlockSpec((1, d), fixed))
        args.append(bias.reshape(1, d))
    in_specs += [pl.BlockSpec((ROW_TILE, d), row), pl.BlockSpec((1, d), fixed)]
    args += [x, g_post.reshape(1, d)]
    out_specs = [pl.BlockSpec((ROW_TILE, d), row)]
    out_shape = [jax.ShapeDtypeStruct((t, d), F32)]
    if has_next:
        in_specs.append(pl.BlockSpec((1, d), fixed))
        args.append(g_next.reshape(1, d))
        out_specs.append(pl.BlockSpec((ROW_TILE, d), row))
        out_shape.append(jax.ShapeDtypeStruct((t, d), BF16))
    outs = pl.pallas_call(
        functools.partial(_proj_res_kernel, has_bias=has_bias, has_next=has_next),
        grid=(t // ROW_TILE,),
        in_specs=in_specs, out_specs=out_specs, out_shape=out_shape,
        compiler_params=_params(("parallel",)),
        name="proj_residual",
    )(*args)
    return (outs[0], outs[1]) if has_next else (outs[0], None)


def _halo_specs(tm, d, n_tiles):
    per = tm // HALO
    last = n_tiles * per - 1
    return [pl.BlockSpec((HALO, d), lambda i, *_: (jnp.maximum(i * per - 1, 0), 0)),
            pl.BlockSpec((tm, d), lambda i, *_: (i, 0)),
            pl.BlockSpec((HALO, d), lambda i, *_: (jnp.minimum((i + 1) * per, last), 0))]


def _seq_edges(i, tiles_per_seq):
    pos = i % tiles_per_seq
    return pos == 0, pos == tiles_per_seq - 1


def _ffn_kernel(*refs, tm, tc, tiles_per_seq, n_chunks, has_next):
    (hp_ref, hm_ref, hn_ref, wg_ref, wv_ref, cg_ref, cv_ref, bg_ref, bv_ref,
     wd_ref, x_ref, gp_ref) = refs[:12]
    pos = 12
    gn_ref = None
    if has_next:
        gn_ref = refs[pos]
        pos += 1
    o_ref = refs[pos]
    pos += 1
    h_ref = None
    if has_next:
        h_ref = refs[pos]
        pos += 1
    hh_ref, acc_ref = refs[pos], refs[pos + 1]
    act_refs = (refs[pos + 2], refs[pos + 3])

    i = pl.program_id(0)
    c = pl.program_id(1)
    rows = (tm + 2 * HALO) // FFN_ROW_BLOCKS

    def up_conv(w_ref, c_ref, b_ref):
        v = jnp.concatenate([_dot(hh_ref[r * rows:(r + 1) * rows, :], w_ref[...])
                             for r in range(FFN_ROW_BLOCKS)], axis=0)
        w = c_ref[...]
        return (w[0:1, :] * v[HALO - 1:HALO - 1 + tm] + w[1:2, :] * v[HALO:HALO + tm]
                + w[2:3, :] * v[HALO + 1:HALO + 1 + tm] + b_ref[...])

    def up_project(dst_ref):
        ug = up_conv(wg_ref, cg_ref, bg_ref)
        uv = up_conv(wv_ref, cv_ref, bv_ref)
        dst_ref[...] = (_gelu_tanh(ug) * uv).astype(BF16)

    @pl.when(c == 0)
    def _():
        first, last = _seq_edges(i, tiles_per_seq)
        keep_prev = jnp.where(first, 0.0, 1.0).astype(BF16)
        keep_next = jnp.where(last, 0.0, 1.0).astype(BF16)
        hh_ref[0:HALO, :] = hp_ref[...] * keep_prev
        hh_ref[HALO:HALO + tm, :] = hm_ref[...]
        hh_ref[HALO + tm:, :] = hn_ref[...] * keep_next
        acc_ref[...] = jnp.zeros_like(acc_ref)
        up_project(act_refs[0])

    for parity in range(2):
        @pl.when((c > 0) & (c < n_chunks) & (c % 2 == parity))
        def _(parity=parity):
            up_project(act_refs[parity])
            acc_ref[...] += _dot(act_refs[1 - parity][...], wd_ref[...])

    @pl.when(c == n_chunks)
    def _():
        m = acc_ref[...] + _dot(act_refs[(n_chunks - 1) % 2][...], wd_ref[...])
        _store_sandwich(m, x_ref, gp_ref, gn_ref, o_ref, h_ref)


def conv_ffn_residual(h, w_up, dw_w, dw_b, w_down, x, g_post, g_next, seq_len):
    t, d = h.shape
    f = w_down.shape[0]
    tm, tc = ROW_TILE, FFN_CHUNK
    n_tiles, n_chunks = t // tm, f // tc
    has_next = g_next is not None
    row = lambda i, c: (i, 0)
    fixed = lambda i, c: (0, 0)
    assert n_chunks % 2 == 0
    up = lambda c: jnp.minimum(c, n_chunks - 1)
    down = lambda c: jnp.maximum(c - 1, 0)
    in_specs = _halo_specs(tm, d, n_tiles) + [
        pl.BlockSpec((d, tc), lambda i, c: (0, up(c))),
        pl.BlockSpec((d, tc), lambda i, c: (0, up(c) + n_chunks)),
        pl.BlockSpec((3, tc), lambda i, c: (0, up(c))),
        pl.BlockSpec((3, tc), lambda i, c: (0, up(c) + n_chunks)),
        pl.BlockSpec((1, tc), lambda i, c: (0, up(c))),
        pl.BlockSpec((1, tc), lambda i, c: (0, up(c) + n_chunks)),
        pl.BlockSpec((tc, d), lambda i, c: (down(c), 0)),
        pl.BlockSpec((tm, d), row),
        pl.BlockSpec((1, d), fixed),
    ]
    dw_b2 = dw_b.reshape(1, 2 * f)
    args = [h, h, h, w_up, w_up, dw_w, dw_w, dw_b2, dw_b2, w_down, x, g_post.reshape(1, d)]
    out_specs = [pl.BlockSpec((tm, d), row)]
    out_shape = [jax.ShapeDtypeStruct((t, d), F32)]
    if has_next:
        in_specs.append(pl.BlockSpec((1, d), fixed))
        args.append(g_next.reshape(1, d))
        out_specs.append(pl.BlockSpec((tm, d), row))
        out_shape.append(jax.ShapeDtypeStruct((t, d), BF16))
    outs = pl.pallas_call(
        functools.partial(_ffn_kernel, tm=tm, tc=tc, tiles_per_seq=seq_len // tm,
                          n_chunks=n_chunks, has_next=has_next),
        grid=(n_tiles, n_chunks + 1),
        in_specs=in_specs, out_specs=out_specs, out_shape=out_shape,
        scratch_shapes=[pltpu.VMEM((tm + 2 * HALO, d), BF16), pltpu.VMEM((tm, d), F32),
                        pltpu.VMEM((tm, tc), BF16), pltpu.VMEM((tm, tc), BF16)],
        compiler_params=_params(("parallel", "arbitrary")),
        name="conv_ffn",
    )(*args)
    return (outs[0], outs[1]) if has_next else (outs[0], None)


def _head_norm_rope(x, gain, cr, sr, cc, sc):
    y = x * lax.rsqrt(jnp.mean(x * x, axis=0, keepdims=True) + EPS) * gain
    f = ROPE_F
    r1, r2, c1, c2 = y[0:f], y[f:2 * f], y[2 * f:3 * f], y[3 * f:4 * f]
    return jnp.concatenate([r1 * cr - r2 * sr, r2 * cr + r1 * sr,
                            c1 * cc - c2 * sc, c2 * cc + c1 * sc], axis=0)


def _qkv_kernel(h_ref, wt_ref, gq_ref, gk_ref, cr_ref, sr_ref, cc_ref, sc_ref,
                q_ref, k_ref, v_ref, *, tn, q_scale):
    h = h_ref[...]
    rope = (cr_ref[...], sr_ref[...], cc_ref[...], sc_ref[...])
    grp = GQA_GROUP * HEAD_DIM
    nt_dims = (((1,), (1,)), ((), ()))

    def proj(row0):
        return lax.dot_general(wt_ref[row0:row0 + grp, :], h, nt_dims,
                               preferred_element_type=F32)

    gq = gq_ref[...]
    for kv in range(N_KV_HEADS):
        r = proj(kv * grp)
        for g in range(GQA_GROUP):
            y = _head_norm_rope(r[g * HEAD_DIM:(g + 1) * HEAD_DIM], gq, *rope) * q_scale
            y = y.astype(BF16)
            for j in range(tn // ATTN_TQ):
                col = (j * GQA_GROUP + g) * ATTN_TQ
                q_ref[kv * HEAD_DIM:(kv + 1) * HEAD_DIM, col:col + ATTN_TQ] = (
                    y[:, j * ATTN_TQ:(j + 1) * ATTN_TQ])
    gk = gk_ref[...]
    r = proj(N_HEADS * HEAD_DIM)
    for kv in range(N_KV_HEADS):
        y = _head_norm_rope(r[kv * HEAD_DIM:(kv + 1) * HEAD_DIM], gk, *rope)
        k_ref[:, kv * HEAD_DIM:(kv + 1) * HEAD_DIM] = y.T.astype(BF16)
    r = proj((N_HEADS + N_KV_HEADS) * HEAD_DIM)
    ones = jnp.ones((SUBLANES_BF16, tn), BF16)
    for kv in range(N_KV_HEADS):
        v_ref[kv, 0, 0:HEAD_DIM, :] = r[kv * HEAD_DIM:(kv + 1) * HEAD_DIM].astype(BF16)
        v_ref[kv, 0, HEAD_DIM:, :] = ones


def qkv_project(h, w_t, q_gain, k_gain, rope_t, seq_len):
    t, d = h.shape
    tn = ATTN_TK
    per_seq = seq_len // tn
    q_scale = HEAD_DIM ** -0.5 * math.log2(math.e)
    gq = jnp.broadcast_to(q_gain.reshape(HEAD_DIM, 1), (HEAD_DIM, tn))
    gk = jnp.broadcast_to(k_gain.reshape(HEAD_DIM, 1), (HEAD_DIM, tn))
    rope_spec = pl.BlockSpec((ROPE_F, tn), lambda i: (0, i % per_seq))
    gain_spec = pl.BlockSpec((HEAD_DIM, tn), lambda i: (0, 0))
    return pl.pallas_call(
        functools.partial(_qkv_kernel, tn=tn, q_scale=q_scale),
        grid=(t // tn,),
        in_specs=[pl.BlockSpec((tn, d), lambda i: (i, 0)),
                  _resident(w_t.shape, lambda i: (0, 0)),
                  gain_spec, gain_spec, rope_spec, rope_spec, rope_spec, rope_spec],
        out_specs=[pl.BlockSpec((N_KV_HEADS * HEAD_DIM, GQA_GROUP * tn), lambda i: (0, i)),
                   pl.BlockSpec((tn, N_KV_HEADS * HEAD_DIM), lambda i: (i, 0)),
                   pl.BlockSpec((N_KV_HEADS, 1, HEAD_DIM + SUBLANES_BF16, tn),
                                lambda i: (0, i, 0, 0))],
        out_shape=[jax.ShapeDtypeStruct((N_KV_HEADS * HEAD_DIM, GQA_GROUP * t), BF16),
                   jax.ShapeDtypeStruct((t, N_KV_HEADS * HEAD_DIM), BF16),
                   jax.ShapeDtypeStruct((N_KV_HEADS, t // tn, HEAD_DIM + SUBLANES_BF16, tn),
                                        BF16)],
        compiler_params=_params(("parallel",)),
        name="qkv_project",
    )(h, w_t, gq, gk, *rope_t)


def _attn_online(q, k_ref, v_ref, m_ref, acc_ref, s_ref, mb_ref, p_ref, al_ref, n_kblocks):
    last_slot = ATTN_SLOTS - 1

    def scores(start, slot):
        s = _dot(k_ref[pl.ds(start, ATTN_TK), :], q)
        s_ref[slot] = s
        mb_ref[slot] = jnp.max(s, axis=0, keepdims=True)

    m_ref[...] = jnp.full_like(m_ref, -jnp.inf)
    acc_ref[...] = jnp.zeros_like(acc_ref)
    p_ref[last_slot] = jnp.zeros(p_ref.shape[1:], p_ref.dtype)
    al_ref[last_slot] = jnp.ones(al_ref.shape[1:], al_ref.dtype)
    scores(0, 0)

    def step(c, slot):
        nxt, prv = (slot + 1) % ATTN_SLOTS, (slot - 1) % ATTN_SLOTS
        c_next = jnp.minimum(c + 1, n_kblocks - 1)
        scores(pl.multiple_of(c_next * ATTN_TK, ATTN_TK), nxt)
        c_prev = jnp.maximum(c - 1, 0)
        acc_ref[...] = acc_ref[...] * al_ref[prv] + _dot(v_ref[0, c_prev], p_ref[prv])
        m_old = m_ref[...]
        m_new = jnp.maximum(m_old, mb_ref[slot])
        p_ref[slot] = jnp.exp2(s_ref[slot] - m_new).astype(p_ref.dtype)
        al_ref[slot] = jnp.exp2(m_old - m_new)
        m_ref[...] = m_new

    def body(i, carry):
        for j in range(ATTN_SLOTS):
            step(ATTN_SLOTS * i + j, j)
        return carry

    lax.fori_loop(0, n_kblocks // ATTN_SLOTS, body, 0)
    acc_ref[...] = (acc_ref[...] * al_ref[last_slot]
                    + _dot(v_ref[0, n_kblocks - 1], p_ref[last_slot]))


def _attn_fixed_shift(q, k_ref, v_ref, acc_ref, shift, n_kblocks):
    acc_ref[...] = jnp.zeros_like(acc_ref)

    def body(i, carry):
        acc = acc_ref[...]
        for j in range(ATTN_SLOTS):
            c = ATTN_SLOTS * i + j
            start = pl.multiple_of(c * ATTN_TK, ATTN_TK)
            s = _dot(k_ref[pl.ds(start, ATTN_TK), :], q)
            acc = acc + _dot(v_ref[0, c], jnp.exp2(s - shift).astype(BF16))
        acc_ref[...] = acc
        return carry

    lax.fori_loop(0, n_kblocks // ATTN_SLOTS, body, 0)


def _attn_kernel(bound_ref, q_ref, k_ref, v_ref, o_ref, m_ref, acc_ref, s_ref, mb_ref, p_ref,
                 al_ref, *, n_kblocks):
    q = q_ref[...]
    bound = bound_ref[0, 0]

    small = bound <= ATTN_MAX_FIXED_SHIFT

    @pl.when(small)
    def _():
        _attn_fixed_shift(q, k_ref, v_ref, acc_ref, bound, n_kblocks)

    @pl.when(jnp.logical_not(small))
    def _():
        _attn_online(q, k_ref, v_ref, m_ref, acc_ref, s_ref, mb_ref, p_ref, al_ref, n_kblocks)

    acc = acc_ref[...]
    out = acc[0:HEAD_DIM] / acc[HEAD_DIM:HEAD_DIM + 1]
    for g in range(GQA_GROUP):
        o_ref[:, g * HEAD_DIM:(g + 1) * HEAD_DIM] = (
            out[:, g * ATTN_TQ:(g + 1) * ATTN_TQ].T.astype(o_ref.dtype))


def score_bound(q_gain, k_gain):
    b = (HEAD_DIM ** 0.5 * math.log2(math.e)) * jnp.max(jnp.abs(q_gain)) * jnp.max(jnp.abs(k_gain))
    return (b * ATTN_BOUND_SLACK).reshape(1, 1).astype(F32)


def flash_attention(q, k, v, bound, batch, seq_len):
    t = batch * seq_len
    q_per_seq = seq_len // ATTN_TQ
    n_kblocks = seq_len // ATTN_TK
    assert n_kblocks % ATTN_SLOTS == 0
    lanes = GQA_GROUP * ATTN_TQ
    return pl.pallas_call(
        functools.partial(_attn_kernel, n_kblocks=n_kblocks),
        grid=(batch, N_KV_HEADS, q_per_seq),
        in_specs=[pl.BlockSpec(memory_space=pltpu.SMEM),
                  pl.BlockSpec((HEAD_DIM, lanes), lambda b, kv, qi: (kv, b * q_per_seq + qi)),
                  pl.BlockSpec((seq_len, HEAD_DIM), lambda b, kv, qi: (b, kv)),
                  pl.BlockSpec((1, n_kblocks, HEAD_DIM + SUBLANES_BF16, ATTN_TK),
                               lambda b, kv, qi: (kv, b, 0, 0))],
        out_specs=pl.BlockSpec((ATTN_TQ, GQA_GROUP * HEAD_DIM),
                               lambda b, kv, qi: (b * q_per_seq + qi, kv)),
        out_shape=jax.ShapeDtypeStruct((t, N_HEADS * HEAD_DIM), BF16),
        scratch_shapes=[pltpu.VMEM((1, lanes), F32),
                        pltpu.VMEM((HEAD_DIM + SUBLANES_BF16, lanes), F32),
                        pltpu.VMEM((ATTN_SLOTS, ATTN_TK, lanes), F32),
                        pltpu.VMEM((ATTN_SLOTS, 1, lanes), F32),
                        pltpu.VMEM((ATTN_SLOTS, ATTN_TK, lanes), BF16),
                        pltpu.VMEM((ATTN_SLOTS, 1, lanes), F32)],
        compiler_params=_params(("parallel", "parallel", "parallel")),
        name="flash_attention",
    )(bound, q, k, v)


def _rope_tables_t(n):
    rows = n // GRID_W
    row = jnp.repeat(jnp.arange(rows, dtype=F32), GRID_W)
    col = jnp.tile(jnp.arange(GRID_W, dtype=F32), rows)
    inv_freq = ROPE_THETA ** (-jnp.arange(0, ROPE_AXIS_DIM, 2, dtype=F32) / ROPE_AXIS_DIM)
    ang_r = inv_freq[:, None] * row[None, :]
    ang_c = inv_freq[:, None] * col[None, :]
    return jnp.cos(ang_r), jnp.sin(ang_r), jnp.cos(ang_c), jnp.sin(ang_c)


def _dft_split(n):
    n2 = max(1, int(round(math.sqrt(n) / 4)))
    while n % n2:
        n2 -= 1
    return n // n2, n2


def _dft_tables(n):
    n1, n2 = _dft_split(n)
    k1 = np.arange(n1)
    ang1 = 2.0 * np.pi * ((k1[:, None] * k1[None, :]) % n1) / n1
    m1 = np.concatenate([np.cos(ang1), -np.sin(ang1)], axis=0)
    t2 = np.arange(n2)
    ang_tw = 2.0 * np.pi * (t2[:, None] * k1[None, :]) / n
    ang2 = 2.0 * np.pi * ((t2[:, None] * t2[None, :]) % n2) / n2
    eye = np.eye(DFT_INTERLEAVE)
    kc = np.einsum("kt,ij->kijt", np.cos(ang2), eye).reshape(n2 * DFT_INTERLEAVE, -1)
    ks = np.einsum("kt,ij->kijt", np.sin(ang2), eye).reshape(n2 * DFT_INTERLEAVE, -1)
    k_re = np.concatenate([kc, ks], axis=1)
    k_im = np.concatenate([-ks, kc], axis=1)
    c = np.arange(FNET_GROUP_DIM)
    ang_c = 2.0 * np.pi * ((c[:, None] * c[None, :]) % FNET_GROUP_DIM) / FNET_GROUP_DIM
    as_bf16 = lambda a: jnp.asarray(a, F32).astype(BF16)
    tw_shape = (n2, n1, LANES)
    return dict(
        n1=n1, n2=n2, m1=as_bf16(m1), k_re=as_bf16(k_re), k_im=as_bf16(k_im),
        cc=as_bf16(np.cos(ang_c)), sc=as_bf16(np.sin(ang_c)),
        tw_cos=jnp.broadcast_to(jnp.asarray(np.cos(ang_tw), F32)[:, :, None], tw_shape),
        tw_sin=jnp.broadcast_to(jnp.asarray(np.sin(ang_tw), F32)[:, :, None], tw_shape))


def _dft1_kernel(x_ref, m1_ref, twc_ref, tws_ref, o_ref, *, n1):
    r = _dot(m1_ref[...], x_ref[...])
    twc, tws = twc_ref[0], tws_ref[0]
    for j in range(x_ref.shape[1] // LANES):
        sl = slice(j * LANES, (j + 1) * LANES)
        ar, ai = r[0:n1, sl], r[n1:, sl]
        o_ref[0, 0, :, sl] = (ar * twc + ai * tws).astype(o_ref.dtype)
        o_ref[0, 1, :, sl] = (ai * twc - ar * tws).astype(o_ref.dtype)


def _dft2_kernel(a_ref, kre_ref, kim_ref, cc_ref, sc_ref, o_ref, *, rows, n2, scale):
    a = a_ref[0].reshape(2 * rows, a_ref.shape[-1])
    zr = _dot(kre_ref[...], a).astype(BF16)
    zi = _dot(kim_ref[...], a).astype(BF16)
    cc, sc = cc_ref[...], sc_ref[...]
    for g in range(FNET_GROUPS):
        sl = slice(g * FNET_GROUP_DIM, (g + 1) * FNET_GROUP_DIM)
        y = (_dot(zr[:, sl], cc) + _dot(zi[:, sl], sc)) * scale
        o_ref[0, :, :, sl] = y.reshape(n2, DFT_INTERLEAVE, FNET_GROUP_DIM)


def fourier_real_2d(h, batch, seq_len, tabs):
    t, d = h.shape
    n1, n2 = tabs["n1"], tabs["n2"]
    rows = n2 * DFT_INTERLEAVE
    a2 = pl.pallas_call(
        functools.partial(_dft1_kernel, n1=n1),
        grid=(batch, n2),
        in_specs=[pl.BlockSpec((n1, d), lambda b, j: (b, j)),
                  _resident((2 * n1, n1), lambda b, j: (0, 0)),
                  pl.BlockSpec((1, n1, LANES), lambda b, j: (j, 0, 0)),
                  pl.BlockSpec((1, n1, LANES), lambda b, j: (j, 0, 0))],
        out_specs=pl.BlockSpec((1, 2, n1, d), lambda b, j: (b, 0, 0, j)),
        out_shape=jax.ShapeDtypeStruct((batch, 2, n1, n2 * d), BF16),
        compiler_params=_params(("parallel", "parallel")),
        name="dft_stage1",
    )(h.reshape(batch * n1, n2 * d), tabs["m1"], tabs["tw_cos"], tabs["tw_sin"])
    scale = 1.0 / math.sqrt(seq_len * FNET_GROUP_DIM)
    y = pl.pallas_call(
        functools.partial(_dft2_kernel, rows=rows, n2=n2, scale=scale),
        grid=(batch, n1 // DFT_INTERLEAVE),
        in_specs=[pl.BlockSpec((1, 2, rows, d), lambda b, j: (b, 0, j, 0)),
                  _resident((rows, 2 * rows), lambda b, j: (0, 0)),
                  _resident((rows, 2 * rows), lambda b, j: (0, 0)),
                  _resident((FNET_GROUP_DIM, FNET_GROUP_DIM), lambda b, j: (0, 0)),
                  _resident((FNET_GROUP_DIM, FNET_GROUP_DIM), lambda b, j: (0, 0))],
        out_specs=pl.BlockSpec((1, n2, DFT_INTERLEAVE, d), lambda b, j: (b, 0, j, 0)),
        out_shape=jax.ShapeDtypeStruct((batch, n2, n1, d), F32),
        compiler_params=_params(("parallel", "parallel")),
        name="dft_stage2",
    )(a2.reshape(batch, 2, n1 * n2, d), tabs["k_re"], tabs["k_im"], tabs["cc"], tabs["sc"])
    return y.reshape(t, d)


def _conf_kernel(hp_ref, hm_ref, hn_ref, w1_ref, b1_ref, dw_ref, db_ref, lg_ref, lb_ref,
                 o_ref, hh_ref, y_ref, *, tm, tiles_per_seq):
    d = o_ref.shape[1]
    i = pl.program_id(0)
    first, last = _seq_edges(i, tiles_per_seq)
    hh_ref[0:HALO, :] = hp_ref[...]
    hh_ref[HALO:HALO + tm, :] = hm_ref[...]
    hh_ref[HALO + tm:, :] = hn_ref[...]
    a = hh_ref[...]
    ext = tm + 2 * HALO
    rows = lax.broadcasted_iota(jnp.int32, (ext, 1), 0)
    lo = jnp.where(first, HALO, 0)
    hi = jnp.where(last, HALO + tm, ext)
    inside = (rows >= lo) & (rows < hi)
    span = tm + SUBLANES_F32 * (2 * HALO // SUBLANES_F32 - 1)
    for j in range(d // CONF_LANE_CHUNK):
        sl = slice(j * CONF_LANE_CHUNK, (j + 1) * CONF_LANE_CHUNK)
        sg = slice(d + j * CONF_LANE_CHUNK, d + (j + 1) * CONF_LANE_CHUNK)
        za = _dot(a, w1_ref[:, sl]) + b1_ref[:, sl]
        zg = _dot(a, w1_ref[:, sg]) + b1_ref[:, sg]
        u = jnp.where(inside, za * _sigmoid(zg), 0.0)
        shifted = [u[b:b + span] for b in range(SUBLANES_F32)]
        y = jnp.zeros((tm, CONF_LANE_CHUNK), F32) + db_ref[:, sl]
        for k in range(CONF_KERNEL):
            off = HALO - CONF_PAD + k
            base = (off // SUBLANES_F32) * SUBLANES_F32
            y = y + dw_ref[k:k + 1, sl] * shifted[off % SUBLANES_F32][base:base + tm]
        y_ref[:, sl] = y
    y = y_ref[...]
    mu = jnp.mean(y, axis=-1, keepdims=True)
    yc = y - mu
    yn = yc * lax.rsqrt(jnp.mean(yc * yc, axis=-1, keepdims=True) + EPS) * lg_ref[...] + lb_ref[...]
    o_ref[...] = (yn * _sigmoid(yn)).astype(o_ref.dtype)


def conformer_inner(h, w1, b1, dw_w, dw_b, ln_g, ln_b, seq_len):
    t, d = h.shape
    tm = CONF_ROW_TILE
    n_tiles = t // tm
    fixed = lambda i: (0, 0)
    return pl.pallas_call(
        functools.partial(_conf_kernel, tm=tm, tiles_per_seq=seq_len // tm),
        grid=(n_tiles,),
        in_specs=_halo_specs(tm, d, n_tiles) + [
            _resident((d, 2 * d), fixed), pl.BlockSpec((1, 2 * d), fixed),
            pl.BlockSpec((CONF_KERNEL, d), fixed), pl.BlockSpec((1, d), fixed),
            pl.BlockSpec((1, d), fixed), pl.BlockSpec((1, d), fixed)],
        out_specs=pl.BlockSpec((tm, d), lambda i: (i, 0)),
        out_shape=jax.ShapeDtypeStruct((t, d), BF16),
        scratch_shapes=[pltpu.VMEM((tm + 2 * HALO, d), BF16), pltpu.VMEM((tm, d), F32)],
        compiler_params=_params(("parallel",)),
        name="conformer_inner",
    )(h, h, h, w1, b1.reshape(1, 2 * d), dw_w, dw_b.reshape(1, d),
      ln_g.reshape(1, d), ln_b.reshape(1, d))


def _run_trunk(x, p):
    batch, seq_len, d = x.shape
    t = batch * seq_len
    xf = x.reshape(t, d)
    rope_t = _rope_tables_t(seq_len)
    tabs = _dft_tables(seq_len)
    h = rmsnorm_bf16(xf, p["norm_mix_pre"][0])
    for i in range(DEPTH):
        j, kind = i // N_MIXERS, i % N_MIXERS
        g_post, g_ffn = p["norm_mix_post"][i], p["norm_ffn_pre"][i]
        if kind == 0:
            q, k, v = qkv_project(h, p["attn_w_qkv_t"][j], p["attn_q_gain"][j],
                                  p["attn_k_gain"][j], rope_t, seq_len)
            bound = score_bound(p["attn_q_gain"][j], p["attn_k_gain"][j])
            o = flash_attention(q, k, v, bound, batch, seq_len)
            xf, h = proj_residual(o, p["attn_w_o"][j], None, xf, g_post, g_ffn)
        elif kind == 1:
            y = fourier_real_2d(h, batch, seq_len, tabs)
            xf, h = proj_residual(y, p["fnet_w_out"][j], p["fnet_b_out"][j], xf, g_post, g_ffn)
        else:
            u = conformer_inner(h, p["conv_w_pw1"][j], p["conv_b_pw1"][j], p["conv_dw_w"][j],
                                p["conv_dw_b"][j], p["conv_ln_g"][j], p["conv_ln_b"][j], seq_len)
            xf, h = proj_residual(u, p["conv_w_pw2"][j], p["conv_b_pw2"][j], xf, g_post, g_ffn)
        g_next = p["norm_mix_pre"][i + 1] if i + 1 < DEPTH else None
        xf, h = conv_ffn_residual(h, p["ffn_w_up"][i], p["ffn_dw_w"][i], p["ffn_dw_b"][i],
                                  p["ffn_w_down"][i], xf, p["norm_ffn_post"][i], g_next, seq_len)
    return xf.reshape(batch, seq_len, d)


def kernel(x_prompt, x_sample, norm_mix_pre, norm_mix_post, norm_ffn_pre, norm_ffn_post, attn_w_qkv, attn_q_gain, attn_k_gain, attn_w_o, fnet_w_out, fnet_b_out, conv_w_pw1, conv_b_pw1, conv_dw_w, conv_dw_b, conv_ln_g, conv_ln_b, conv_w_pw2, conv_b_pw2, ffn_w_up, ffn_dw_w, ffn_dw_b, ffn_w_down):
    def per_layer_bf16(w):
        return [w[j].astype(BF16) for j in range(w.shape[0])]

    p = {
        "norm_mix_pre": norm_mix_pre, "norm_mix_post": norm_mix_post,
        "norm_ffn_pre": norm_ffn_pre, "norm_ffn_post": norm_ffn_post,
        "attn_w_qkv_t": [w.T.astype(BF16) for w in attn_w_qkv],
        "attn_q_gain": attn_q_gain, "attn_k_gain": attn_k_gain,
        "attn_w_o": per_layer_bf16(attn_w_o),
        "fnet_w_out": per_layer_bf16(fnet_w_out), "fnet_b_out": fnet_b_out,
        "conv_w_pw1": per_layer_bf16(conv_w_pw1), "conv_b_pw1": conv_b_pw1,
        "conv_dw_w": conv_dw_w, "conv_dw_b": conv_dw_b,
        "conv_ln_g": conv_ln_g, "conv_ln_b": conv_ln_b,
        "conv_w_pw2": per_layer_bf16(conv_w_pw2), "conv_b_pw2": conv_b_pw2,
        "ffn_w_up": per_layer_bf16(ffn_w_up), "ffn_dw_w": ffn_dw_w, "ffn_dw_b": ffn_dw_b,
        "ffn_w_down": per_layer_bf16(ffn_w_down),
    }
    return (_run_trunk(x_prompt, p), _run_trunk(x_sample, p))
```

```python
import functools
import math

import numpy as np
import jax
import jax.numpy as jnp
from jax import lax
from jax.experimental import pallas as pl
from jax.experimental.pallas import tpu as pltpu

F32 = jnp.float32
BF16 = jnp.bfloat16

D_MODEL = 2048
DEPTH = 4
N_MIXERS = 3
GRID_W = 64
HEAD_DIM = 128
N_HEADS = D_MODEL // HEAD_DIM
N_KV_HEADS = N_HEADS // 4
GQA_GROUP = N_HEADS // N_KV_HEADS
ROPE_THETA = 10000.0
ROPE_AXIS_DIM = HEAD_DIM // 2
ROPE_F = ROPE_AXIS_DIM // 2
FNET_GROUPS = 4
FNET_GROUP_DIM = D_MODEL // FNET_GROUPS
CONF_KERNEL = 31
CONF_PAD = (CONF_KERNEL - 1) // 2
FFN_DIM = 4 * D_MODEL
EPS = 1e-6

V7X_VMEM_BYTES = 64 * 1024 * 1024
V7X_VMEM_LIMIT = V7X_VMEM_BYTES - 8 * 1024 * 1024
LANES = 128
SUBLANES_F32 = 8
SUBLANES_BF16 = 16
HALO = SUBLANES_BF16

ROW_TILE = 512
FFN_CHUNK = 512
FFN_ROW_BLOCKS = 2
CONF_ROW_TILE = 256
CONF_LANE_CHUNK = 512
CONF_ROW_BLOCK = 32
ATTN_TQ = 256
ATTN_TK = 512
ATTN_SLOTS = 4
ATTN_MAX_FIXED_SHIFT = 60.0
ATTN_BOUND_SLACK = 1.02
DFT_INTERLEAVE = SUBLANES_F32


def _params(semantics):
    return pltpu.CompilerParams(dimension_semantics=semantics,
                                vmem_limit_bytes=V7X_VMEM_LIMIT)


def _resident(shape, index_map):
    return pl.BlockSpec(shape, index_map, pipeline_mode=pl.Buffered(1))


def _rms(x, g):
    return x * lax.rsqrt(jnp.mean(x * x, axis=-1, keepdims=True) + EPS) * g


def _sigmoid(x):
    return 1.0 / (1.0 + jnp.exp(-x))


def _gelu_tanh(x):
    a = -2.0 * math.sqrt(2.0 / math.pi) * math.log2(math.e)
    return x / (1.0 + jnp.exp2(x * (a + (a * 0.044715) * (x * x))))


def _dot(a, b):
    return jnp.dot(a, b, preferred_element_type=F32)


def _store_sandwich(m, x_ref, gp_ref, gn_ref, o_ref, h_ref):
    xn = x_ref[...] + _rms(m, gp_ref[...])
    o_ref[...] = xn
    if h_ref is not None:
        h_ref[...] = _rms(xn, gn_ref[...]).astype(BF16)


def _rmsnorm_kernel(x_ref, g_ref, o_ref):
    o_ref[...] = _rms(x_ref[...], g_ref[...]).astype(o_ref.dtype)


def rmsnorm_bf16(x, g):
    t, d = x.shape
    return pl.pallas_call(
        _rmsnorm_kernel,
        grid=(t // ROW_TILE,),
        in_specs=[pl.BlockSpec((ROW_TILE, d), lambda i: (i, 0)),
                  pl.BlockSpec((1, d), lambda i: (0, 0))],
        out_specs=pl.BlockSpec((ROW_TILE, d), lambda i: (i, 0)),
        out_shape=jax.ShapeDtypeStruct((t, d), BF16),
        compiler_params=_params(("parallel",)),
        name="rmsnorm",
    )(x, g.reshape(1, d))


def _proj_res_kernel(*refs, has_bias, has_next):
    a_ref, w_ref = refs[0], refs[1]
    pos = 2
    b_ref = None
    if has_bias:
        b_ref = refs[pos]
        pos += 1
    x_ref, gp_ref = refs[pos], refs[pos + 1]
    pos += 2
    gn_ref = None
    if has_next:
        gn_ref = refs[pos]
        pos += 1
    o_ref = refs[pos]
    h_ref = refs[pos + 1] if has_next else None
    m = _dot(a_ref[...].astype(BF16), w_ref[...])
    if has_bias:
        m = m + b_ref[...]
    _store_sandwich(m, x_ref, gp_ref, gn_ref, o_ref, h_ref)


def proj_residual(a, w, layer, bias, x, g_post, g_next):
    t, k = a.shape
    d = w.shape[2]
    has_bias = bias is not None
    has_next = g_next is not None
    row = lambda i: (i, 0)
    fixed = lambda i: (0, 0)
    in_specs = [pl.BlockSpec((ROW_TILE, k), row),
                _resident((None, k, d), lambda i: (layer, 0, 0))]
    args = [a, w]
    if has_bias:
        in_specs.append(pl.BlockSpec((1, d), fixed))
        args.append(bias.reshape(1, d))
    in_specs += [pl.BlockSpec((ROW_TILE, d), row), pl.BlockSpec((1, d), fixed)]
    args += [x, g_post.reshape(1, d)]
    out_specs = [pl.BlockSpec((ROW_TILE, d), row)]
    out_shape = [jax.ShapeDtypeStruct((t, d), F32)]
    if has_next:
        in_specs.append(pl.BlockSpec((1, d), fixed))
        args.append(g_next.reshape(1, d))
        out_specs.append(pl.BlockSpec((ROW_TILE, d), row))
        out_shape.append(jax.ShapeDtypeStruct((t, d), BF16))
    outs = pl.pallas_call(
        functools.partial(_proj_res_kernel, has_bias=has_bias, has_next=has_next),
        grid=(t // ROW_TILE,),
        in_specs=in_specs, out_specs=out_specs, out_shape=out_shape,
        compiler_params=_params(("parallel",)),
        name="proj_residual",
    )(*args)
    return (outs[0], outs[1]) if has_next else (outs[0], None)


def _halo_specs(tm, d, n_tiles):
    per = tm // HALO
    last = n_tiles * per - 1
    return [pl.BlockSpec((HALO, d), lambda i, *_: (jnp.maximum(i * per - 1, 0), 0)),
            pl.BlockSpec((tm, d), lambda i, *_: (i, 0)),
            pl.BlockSpec((HALO, d), lambda i, *_: (jnp.minimum((i + 1) * per, last), 0))]


def _seq_edges(i, tiles_per_seq):
    pos = i % tiles_per_seq
    return pos == 0, pos == tiles_per_seq - 1


def _ffn_kernel(*refs, tm, tc, tiles_per_seq, n_chunks, has_next):
    (hp_ref, hm_ref, hn_ref, wg_ref, wv_ref, cg_ref, cv_ref, bg_ref, bv_ref,
     wd_ref, x_ref, gp_ref) = refs[:12]
    pos = 12
    gn_ref = None
    if has_next:
        gn_ref = refs[pos]
        pos += 1
    o_ref = refs[pos]
    pos += 1
    h_ref = None
    if has_next:
        h_ref = refs[pos]
        pos += 1
    hh_ref, acc_ref = refs[pos], refs[pos + 1]
    act_refs = (refs[pos + 2], refs[pos + 3])

    i = pl.program_id(0)
    c = pl.program_id(1)
    rows = (tm + 2 * HALO) // FFN_ROW_BLOCKS

    def up_conv(w_ref, c_ref, b_ref):
        v = jnp.concatenate([_dot(hh_ref[r * rows:(r + 1) * rows, :], w_ref[...])
                             for r in range(FFN_ROW_BLOCKS)], axis=0)
        w = c_ref[...]
        return (w[0:1, :] * v[HALO - 1:HALO - 1 + tm] + w[1:2, :] * v[HALO:HALO + tm]
                + w[2:3, :] * v[HALO + 1:HALO + 1 + tm] + b_ref[...])

    def up_project(dst_ref):
        ug = up_conv(wg_ref, cg_ref, bg_ref)
        uv = up_conv(wv_ref, cv_ref, bv_ref)
        dst_ref[...] = (_gelu_tanh(ug) * uv).astype(BF16)

    @pl.when(c == 0)
    def _():
        first, last = _seq_edges(i, tiles_per_seq)
        keep_prev = jnp.where(first, 0.0, 1.0).astype(BF16)
        keep_next = jnp.where(last, 0.0, 1.0).astype(BF16)
        hh_ref[0:HALO, :] = hp_ref[...] * keep_prev
        hh_ref[HALO:HALO + tm, :] = hm_ref[...]
        hh_ref[HALO + tm:, :] = hn_ref[...] * keep_next
        acc_ref[...] = jnp.zeros_like(acc_ref)
        up_project(act_refs[0])

    for parity in range(2):
        @pl.when((c > 0) & (c < n_chunks) & (c % 2 == parity))
        def _(parity=parity):
            up_project(act_refs[parity])
            acc_ref[...] += _dot(act_refs[1 - parity][...], wd_ref[...])

    @pl.when(c == n_chunks)
    def _():
        m = acc_ref[...] + _dot(act_refs[(n_chunks - 1) % 2][...], wd_ref[...])
        _store_sandwich(m, x_ref, gp_ref, gn_ref, o_ref, h_ref)


def ffn_up_chunked(w_up):
    n_layers, d, f2 = w_up.shape
    return w_up.astype(BF16).reshape(n_layers, d, f2 // FFN_CHUNK, FFN_CHUNK).transpose(0, 2, 1, 3)


def conv_ffn_residual(h, w_up_c, dw_w, dw_b, w_down, layer, x, g_post, g_next, seq_len):
    t, d = h.shape
    f = w_down.shape[1]
    tm, tc = ROW_TILE, FFN_CHUNK
    n_tiles, n_chunks = t // tm, f // tc
    has_next = g_next is not None
    row = lambda i, c: (i, 0)
    fixed = lambda i, c: (0, 0)
    assert n_chunks % 2 == 0
    up = lambda c: jnp.minimum(c, n_chunks - 1)
    down = lambda c: jnp.maximum(c - 1, 0)
    in_specs = _halo_specs(tm, d, n_tiles) + [
        pl.BlockSpec((None, None, d, tc), lambda i, c: (layer, up(c), 0, 0)),
        pl.BlockSpec((None, None, d, tc), lambda i, c: (layer, up(c) + n_chunks, 0, 0)),
        pl.BlockSpec((None, 3, tc), lambda i, c: (layer, 0, up(c))),
        pl.BlockSpec((None, 3, tc), lambda i, c: (layer, 0, up(c) + n_chunks)),
        pl.BlockSpec((None, 1, tc), lambda i, c: (layer, 0, up(c))),
        pl.BlockSpec((None, 1, tc), lambda i, c: (layer, 0, up(c) + n_chunks)),
        pl.BlockSpec((None, tc, d), lambda i, c: (layer, down(c), 0)),
        pl.BlockSpec((tm, d), row),
        pl.BlockSpec((1, d), fixed),
    ]
    dw_b3 = dw_b.reshape(dw_b.shape[0], 1, 2 * f)
    args = [h, h, h, w_up_c, w_up_c, dw_w, dw_w, dw_b3, dw_b3, w_down, x, g_post.reshape(1, d)]
    out_specs = [pl.BlockSpec((tm, d), row)]
    out_shape = [jax.ShapeDtypeStruct((t, d), F32)]
    if has_next:
        in_specs.append(pl.BlockSpec((1, d), fixed))
        args.append(g_next.reshape(1, d))
        out_specs.append(pl.BlockSpec((tm, d), row))
        out_shape.append(jax.ShapeDtypeStruct((t, d), BF16))
    outs = pl.pallas_call(
        functools.partial(_ffn_kernel, tm=tm, tc=tc, tiles_per_seq=seq_len // tm,
                          n_chunks=n_chunks, has_next=has_next),
        grid=(n_tiles, n_chunks + 1),
        in_specs=in_specs, out_specs=out_specs, out_shape=out_shape,
        scratch_shapes=[pltpu.VMEM((tm + 2 * HALO, d), BF16), pltpu.VMEM((tm, d), F32),
                        pltpu.VMEM((tm, tc), BF16), pltpu.VMEM((tm, tc), BF16)],
        compiler_params=_params(("parallel", "arbitrary")),
        name="conv_ffn",
    )(*args)
    return (outs[0], outs[1]) if has_next else (outs[0], None)


def _head_norm_rope(x, gain, cr, sr, cc, sc):
    y = x * lax.rsqrt(jnp.mean(x * x, axis=0, keepdims=True) + EPS) * gain
    f = ROPE_F
    r1, r2, c1, c2 = y[0:f], y[f:2 * f], y[2 * f:3 * f], y[3 * f:4 * f]
    return jnp.concatenate([r1 * cr - r2 * sr, r2 * cr + r1 * sr,
                            c1 * cc - c2 * sc, c2 * cc + c1 * sc], axis=0)


def _qkv_kernel(h_ref, wt_ref, gq_ref, gk_ref, cr_ref, sr_ref, cc_ref, sc_ref,
                q_ref, k_ref, v_ref, *, tn, q_scale):
    h = h_ref[...]
    rope = (cr_ref[...], sr_ref[...], cc_ref[...], sc_ref[...])
    grp = GQA_GROUP * HEAD_DIM
    nt_dims = (((1,), (1,)), ((), ()))

    def proj(row0):
        return lax.dot_general(wt_ref[row0:row0 + grp, :], h, nt_dims,
                               preferred_element_type=F32)

    gq = gq_ref[...]
    for kv in range(N_KV_HEADS):
        r = proj(kv * grp)
        for g in range(GQA_GROUP):
            y = _head_norm_rope(r[g * HEAD_DIM:(g + 1) * HEAD_DIM], gq, *rope) * q_scale
            y = y.astype(BF16)
            for j in range(tn // ATTN_TQ):
                col = (j * GQA_GROUP + g) * ATTN_TQ
                q_ref[kv * HEAD_DIM:(kv + 1) * HEAD_DIM, col:col + ATTN_TQ] = (
                    y[:, j * ATTN_TQ:(j + 1) * ATTN_TQ])
    gk = gk_ref[...]
    r = proj(N_HEADS * HEAD_DIM)
    for kv in range(N_KV_HEADS):
        y = _head_norm_rope(r[kv * HEAD_DIM:(kv + 1) * HEAD_DIM], gk, *rope)
        k_ref[:, kv * HEAD_DIM:(kv + 1) * HEAD_DIM] = y.T.astype(BF16)
    r = proj((N_HEADS + N_KV_HEADS) * HEAD_DIM)
    ones = jnp.ones((SUBLANES_BF16, tn), BF16)
    for kv in range(N_KV_HEADS):
        v_ref[kv, 0, 0:HEAD_DIM, :] = r[kv * HEAD_DIM:(kv + 1) * HEAD_DIM].astype(BF16)
        v_ref[kv, 0, HEAD_DIM:, :] = ones


def qkv_project(h, w_t, layer, q_gain, k_gain, rope_t, seq_len):
    t, d = h.shape
    tn = ATTN_TK
    per_seq = seq_len // tn
    q_scale = HEAD_DIM ** -0.5 * math.log2(math.e)
    gq = jnp.broadcast_to(q_gain.reshape(HEAD_DIM, 1), (HEAD_DIM, tn))
    gk = jnp.broadcast_to(k_gain.reshape(HEAD_DIM, 1), (HEAD_DIM, tn))
    rope_spec = pl.BlockSpec((ROPE_F, tn), lambda i: (0, i % per_seq))
    gain_spec = pl.BlockSpec((HEAD_DIM, tn), lambda i: (0, 0))
    return pl.pallas_call(
        functools.partial(_qkv_kernel, tn=tn, q_scale=q_scale),
        grid=(t // tn,),
        in_specs=[pl.BlockSpec((tn, d), lambda i: (i, 0)),
                  _resident((None,) + w_t.shape[1:], lambda i: (layer, 0, 0)),
                  gain_spec, gain_spec, rope_spec, rope_spec, rope_spec, rope_spec],
        out_specs=[pl.BlockSpec((N_KV_HEADS * HEAD_DIM, GQA_GROUP * tn), lambda i: (0, i)),
                   pl.BlockSpec((tn, N_KV_HEADS * HEAD_DIM), lambda i: (i, 0)),
                   pl.BlockSpec((N_KV_HEADS, 1, HEAD_DIM + SUBLANES_BF16, tn),
                                lambda i: (0, i, 0, 0))],
        out_shape=[jax.ShapeDtypeStruct((N_KV_HEADS * HEAD_DIM, GQA_GROUP * t), BF16),
                   jax.ShapeDtypeStruct((t, N_KV_HEADS * HEAD_DIM), BF16),
                   jax.ShapeDtypeStruct((N_KV_HEADS, t // tn, HEAD_DIM + SUBLANES_BF16, tn),
                                        BF16)],
        compiler_params=_params(("parallel",)),
        name="qkv_project",
    )(h, w_t, gq, gk, *rope_t)


def _attn_online(q, k_ref, v_ref, m_ref, acc_ref, s_ref, mb_ref, p_ref, al_ref, n_kblocks):
    last_slot = ATTN_SLOTS - 1

    def scores(start, slot):
        s = _dot(k_ref[pl.ds(start, ATTN_TK), :], q)
        s_ref[slot] = s
        mb_ref[slot] = jnp.max(s, axis=0, keepdims=True)

    m_ref[...] = jnp.full_like(m_ref, -jnp.inf)
    acc_ref[...] = jnp.zeros_like(acc_ref)
    p_ref[last_slot] = jnp.zeros(p_ref.shape[1:], p_ref.dtype)
    al_ref[last_slot] = jnp.ones(al_ref.shape[1:], al_ref.dtype)
    scores(0, 0)

    def step(c, slot):
        nxt, prv = (slot + 1) % ATTN_SLOTS, (slot - 1) % ATTN_SLOTS
        c_next = jnp.minimum(c + 1, n_kblocks - 1)
        scores(pl.multiple_of(c_next * ATTN_TK, ATTN_TK), nxt)
        c_prev = jnp.maximum(c - 1, 0)
        acc_ref[...] = acc_ref[...] * al_ref[prv] + _dot(v_ref[0, c_prev], p_ref[prv])
        m_old = m_ref[...]
        m_new = jnp.maximum(m_old, mb_ref[slot])
        p_ref[slot] = jnp.exp2(s_ref[slot] - m_new).astype(p_ref.dtype)
        al_ref[slot] = jnp.exp2(m_old - m_new)
        m_ref[...] = m_new

    def body(i, carry):
        for j in range(ATTN_SLOTS):
            step(ATTN_SLOTS * i + j, j)
        return carry

    lax.fori_loop(0, n_kblocks // ATTN_SLOTS, body, 0)
    acc_ref[...] = (acc_ref[...] * al_ref[last_slot]
                    + _dot(v_ref[0, n_kblocks - 1], p_ref[last_slot]))


def _attn_fixed_shift(q, k_ref, v_ref, acc_ref, shift, n_kblocks):
    acc_ref[...] = jnp.zeros_like(acc_ref)

    def body(i, carry):
        acc = acc_ref[...]
        for j in range(ATTN_SLOTS):
            c = ATTN_SLOTS * i + j
            start = pl.multiple_of(c * ATTN_TK, ATTN_TK)
            s = _dot(k_ref[pl.ds(start, ATTN_TK), :], q)
            acc = acc + _dot(v_ref[0, c], jnp.exp2(s - shift).astype(BF16))
        acc_ref[...] = acc
        return carry

    lax.fori_loop(0, n_kblocks // ATTN_SLOTS, body, 0)


def _attn_kernel(bound_ref, q_ref, k_ref, v_ref, o_ref, m_ref, acc_ref, s_ref, mb_ref, p_ref,
                 al_ref, *, n_kblocks):
    q = q_ref[...]
    bound = bound_ref[0, 0]

    small = bound <= ATTN_MAX_FIXED_SHIFT

    @pl.when(small)
    def _():
        _attn_fixed_shift(q, k_ref, v_ref, acc_ref, bound, n_kblocks)

    @pl.when(jnp.logical_not(small))
    def _():
        _attn_online(q, k_ref, v_ref, m_ref, acc_ref, s_ref, mb_ref, p_ref, al_ref, n_kblocks)

    acc = acc_ref[...]
    out = acc[0:HEAD_DIM] / acc[HEAD_DIM:HEAD_DIM + 1]
    for g in range(GQA_GROUP):
        o_ref[:, g * HEAD_DIM:(g + 1) * HEAD_DIM] = (
            out[:, g * ATTN_TQ:(g + 1) * ATTN_TQ].T.astype(o_ref.dtype))


def score_bound(q_gain, k_gain):
    b = (HEAD_DIM ** 0.5 * math.log2(math.e)) * jnp.max(jnp.abs(q_gain)) * jnp.max(jnp.abs(k_gain))
    return (b * ATTN_BOUND_SLACK).reshape(1, 1).astype(F32)


def flash_attention(q, k, v, bound, batch, seq_len):
    t = batch * seq_len
    q_per_seq = seq_len // ATTN_TQ
    n_kblocks = seq_len // ATTN_TK
    assert n_kblocks % ATTN_SLOTS == 0
    lanes = GQA_GROUP * ATTN_TQ
    return pl.pallas_call(
        functools.partial(_attn_kernel, n_kblocks=n_kblocks),
        grid=(batch, N_KV_HEADS, q_per_seq),
        in_specs=[pl.BlockSpec(memory_space=pltpu.SMEM),
                  pl.BlockSpec((HEAD_DIM, lanes), lambda b, kv, qi: (kv, b * q_per_seq + qi)),
                  pl.BlockSpec((seq_len, HEAD_DIM), lambda b, kv, qi: (b, kv)),
                  pl.BlockSpec((1, n_kblocks, HEAD_DIM + SUBLANES_BF16, ATTN_TK),
                               lambda b, kv, qi: (kv, b, 0, 0))],
        out_specs=pl.BlockSpec((ATTN_TQ, GQA_GROUP * HEAD_DIM),
                               lambda b, kv, qi: (b * q_per_seq + qi, kv)),
        out_shape=jax.ShapeDtypeStruct((t, N_HEADS * HEAD_DIM), BF16),
        scratch_shapes=[pltpu.VMEM((1, lanes), F32),
                        pltpu.VMEM((HEAD_DIM + SUBLANES_BF16, lanes), F32),
                        pltpu.VMEM((ATTN_SLOTS, ATTN_TK, lanes), F32),
                        pltpu.VMEM((ATTN_SLOTS, 1, lanes), F32),
                        pltpu.VMEM((ATTN_SLOTS, ATTN_TK, lanes), BF16),
                        pltpu.VMEM((ATTN_SLOTS, 1, lanes), F32)],
        compiler_params=_params(("parallel", "parallel", "parallel")),
        name="flash_attention",
    )(bound, q, k, v)


def _rope_tables_t(n):
    rows = n // GRID_W
    row = jnp.repeat(jnp.arange(rows, dtype=F32), GRID_W)
    col = jnp.tile(jnp.arange(GRID_W, dtype=F32), rows)
    inv_freq = ROPE_THETA ** (-jnp.arange(0, ROPE_AXIS_DIM, 2, dtype=F32) / ROPE_AXIS_DIM)
    ang_r = inv_freq[:, None] * row[None, :]
    ang_c = inv_freq[:, None] * col[None, :]
    return jnp.cos(ang_r), jnp.sin(ang_r), jnp.cos(ang_c), jnp.sin(ang_c)


def _dft_split(n):
    n2 = max(1, int(round(math.sqrt(n) / 4)))
    while n % n2:
        n2 -= 1
    return n // n2, n2


def _dft_tables(n):
    n1, n2 = _dft_split(n)
    k1 = np.arange(n1)
    ang1 = 2.0 * np.pi * ((k1[:, None] * k1[None, :]) % n1) / n1
    m1 = np.concatenate([np.cos(ang1), -np.sin(ang1)], axis=0)
    t2 = np.arange(n2)
    ang_tw = 2.0 * np.pi * (t2[:, None] * k1[None, :]) / n
    ang2 = 2.0 * np.pi * ((t2[:, None] * t2[None, :]) % n2) / n2
    eye = np.eye(DFT_INTERLEAVE)
    kc = np.einsum("kt,ij->kijt", np.cos(ang2), eye).reshape(n2 * DFT_INTERLEAVE, -1)
    ks = np.einsum("kt,ij->kijt", np.sin(ang2), eye).reshape(n2 * DFT_INTERLEAVE, -1)
    k_re = np.concatenate([kc, ks], axis=1)
    k_im = np.concatenate([-ks, kc], axis=1)
    c = np.arange(FNET_GROUP_DIM)
    ang_c = 2.0 * np.pi * ((c[:, None] * c[None, :]) % FNET_GROUP_DIM) / FNET_GROUP_DIM
    as_bf16 = lambda a: jnp.asarray(a, F32).astype(BF16)
    tw_shape = (n2, n1, LANES)
    return dict(
        n1=n1, n2=n2, m1=as_bf16(m1), k_re=as_bf16(k_re), k_im=as_bf16(k_im),
        cc=as_bf16(np.cos(ang_c)), sc=as_bf16(np.sin(ang_c)),
        tw_cos=jnp.broadcast_to(jnp.asarray(np.cos(ang_tw), F32)[:, :, None], tw_shape),
        tw_sin=jnp.broadcast_to(jnp.asarray(np.sin(ang_tw), F32)[:, :, None], tw_shape))


def _dft1_kernel(x_ref, m1_ref, twc_ref, tws_ref, o_ref, *, n1):
    r = _dot(m1_ref[...], x_ref[...])
    twc, tws = twc_ref[0], tws_ref[0]
    for j in range(x_ref.shape[1] // LANES):
        sl = slice(j * LANES, (j + 1) * LANES)
        ar, ai = r[0:n1, sl], r[n1:, sl]
        o_ref[0, 0, :, sl] = (ar * twc + ai * tws).astype(o_ref.dtype)
        o_ref[0, 1, :, sl] = (ai * twc - ar * tws).astype(o_ref.dtype)


def _dft2_kernel(a_ref, kre_ref, kim_ref, cc_ref, sc_ref, o_ref, *, rows, n2, scale):
    a = a_ref[0].reshape(2 * rows, a_ref.shape[-1])
    zr = _dot(kre_ref[...], a).astype(BF16)
    zi = _dot(kim_ref[...], a).astype(BF16)
    cc, sc = cc_ref[...], sc_ref[...]
    for g in range(FNET_GROUPS):
        sl = slice(g * FNET_GROUP_DIM, (g + 1) * FNET_GROUP_DIM)
        y = (_dot(zr[:, sl], cc) + _dot(zi[:, sl], sc)) * scale
        o_ref[0, :, :, sl] = y.reshape(n2, DFT_INTERLEAVE, FNET_GROUP_DIM)


def fourier_real_2d(h, batch, seq_len, tabs):
    t, d = h.shape
    n1, n2 = tabs["n1"], tabs["n2"]
    rows = n2 * DFT_INTERLEAVE
    a2 = pl.pallas_call(
        functools.partial(_dft1_kernel, n1=n1),
        grid=(batch, n2),
        in_specs=[pl.BlockSpec((n1, d), lambda b, j: (b, j)),
                  _resident((2 * n1, n1), lambda b, j: (0, 0)),
                  pl.BlockSpec((1, n1, LANES), lambda b, j: (j, 0, 0)),
                  pl.BlockSpec((1, n1, LANES), lambda b, j: (j, 0, 0))],
        out_specs=pl.BlockSpec((1, 2, n1, d), lambda b, j: (b, 0, 0, j)),
        out_shape=jax.ShapeDtypeStruct((batch, 2, n1, n2 * d), BF16),
        compiler_params=_params(("parallel", "parallel")),
        name="dft_stage1",
    )(h.reshape(batch * n1, n2 * d), tabs["m1"], tabs["tw_cos"], tabs["tw_sin"])
    scale = 1.0 / math.sqrt(seq_len * FNET_GROUP_DIM)
    y = pl.pallas_call(
        functools.partial(_dft2_kernel, rows=rows, n2=n2, scale=scale),
        grid=(batch, n1 // DFT_INTERLEAVE),
        in_specs=[pl.BlockSpec((1, 2, rows, d), lambda b, j: (b, 0, j, 0)),
                  _resident((rows, 2 * rows), lambda b, j: (0, 0)),
                  _resident((rows, 2 * rows), lambda b, j: (0, 0)),
                  _resident((FNET_GROUP_DIM, FNET_GROUP_DIM), lambda b, j: (0, 0)),
                  _resident((FNET_GROUP_DIM, FNET_GROUP_DIM), lambda b, j: (0, 0))],
        out_specs=pl.BlockSpec((1, n2, DFT_INTERLEAVE, d), lambda b, j: (b, 0, j, 0)),
        out_shape=jax.ShapeDtypeStruct((batch, n2, n1, d), F32),
        compiler_params=_params(("parallel", "parallel")),
        name="dft_stage2",
    )(a2.reshape(batch, 2, n1 * n2, d), tabs["k_re"], tabs["k_im"], tabs["cc"], tabs["sc"])
    return y.reshape(t, d)


def _conf_kernel(hp_ref, hm_ref, hn_ref, w1_ref, b1_ref, dw_ref, db_ref, lg_ref, lb_ref,
                 o_ref, hh_ref, y_ref, sh_ref, *, tm, tiles_per_seq):
    d = o_ref.shape[1]
    i = pl.program_id(0)
    first, last = _seq_edges(i, tiles_per_seq)
    hh_ref[0:HALO, :] = hp_ref[...]
    hh_ref[HALO:HALO + tm, :] = hm_ref[...]
    hh_ref[HALO + tm:, :] = hn_ref[...]
    a = hh_ref[...]
    ext = tm + 2 * HALO
    rows = lax.broadcasted_iota(jnp.int32, (ext, 1), 0)
    lo = jnp.where(first, HALO, 0)
    hi = jnp.where(last, HALO + tm, ext)
    inside = (rows >= lo) & (rows < hi)
    span = tm + SUBLANES_F32 * (2 * HALO // SUBLANES_F32 - 1)
    for j in range(d // CONF_LANE_CHUNK):
        sl = slice(j * CONF_LANE_CHUNK, (j + 1) * CONF_LANE_CHUNK)
        sg = slice(d + j * CONF_LANE_CHUNK, d + (j + 1) * CONF_LANE_CHUNK)
        za = _dot(a, w1_ref[:, sl]) + b1_ref[:, sl]
        zg = _dot(a, w1_ref[:, sg]) + b1_ref[:, sg]
        u = jnp.where(inside, za * _sigmoid(zg), 0.0)
        for b in range(SUBLANES_F32):
            sh_ref[b] = u[b:b + span]
        sub = CONF_ROW_BLOCK // SUBLANES_F32
        for cc in range(CONF_LANE_CHUNK // LANES):
            lanes = slice(cc * LANES, (cc + 1) * LANES)
            col = slice(j * CONF_LANE_CHUNK + cc * LANES, j * CONF_LANE_CHUNK + (cc + 1) * LANES)
            taps = [jnp.broadcast_to(dw_ref[k:k + 1, col], (SUBLANES_F32, LANES))
                    for k in range(CONF_KERNEL)]
            bias = jnp.broadcast_to(db_ref[:, col], (SUBLANES_F32, LANES))
            for r0 in range(0, tm, CONF_ROW_BLOCK):
                acc = jnp.broadcast_to(bias[None], (sub, SUBLANES_F32, LANES))
                for k in range(CONF_KERNEL):
                    off = HALO - CONF_PAD + k
                    base = (off // SUBLANES_F32) * SUBLANES_F32 + r0
                    win = sh_ref[off % SUBLANES_F32, base:base + CONF_ROW_BLOCK, lanes]
                    acc = acc + taps[k][None] * win.reshape(sub, SUBLANES_F32, LANES)
                y_ref[r0:r0 + CONF_ROW_BLOCK, col] = acc.reshape(CONF_ROW_BLOCK, LANES)
    y = y_ref[...]
    mu = jnp.mean(y, axis=-1, keepdims=True)
    yc = y - mu
    yn = yc * lax.rsqrt(jnp.mean(yc * yc, axis=-1, keepdims=True) + EPS) * lg_ref[...] + lb_ref[...]
    o_ref[...] = (yn * _sigmoid(yn)).astype(o_ref.dtype)


def conformer_inner(h, w1, layer, b1, dw_w, dw_b, ln_g, ln_b, seq_len):
    t, d = h.shape
    tm = CONF_ROW_TILE
    n_tiles = t // tm
    fixed = lambda i: (0, 0)
    return pl.pallas_call(
        functools.partial(_conf_kernel, tm=tm, tiles_per_seq=seq_len // tm),
        grid=(n_tiles,),
        in_specs=_halo_specs(tm, d, n_tiles) + [
            _resident((None, d, 2 * d), lambda i: (layer, 0, 0)), pl.BlockSpec((1, 2 * d), fixed),
            pl.BlockSpec((CONF_KERNEL, d), fixed), pl.BlockSpec((1, d), fixed),
            pl.BlockSpec((1, d), fixed), pl.BlockSpec((1, d), fixed)],
        out_specs=pl.BlockSpec((tm, d), lambda i: (i, 0)),
        out_shape=jax.ShapeDtypeStruct((t, d), BF16),
        scratch_shapes=[pltpu.VMEM((tm + 2 * HALO, d), BF16), pltpu.VMEM((tm, d), F32),
                        pltpu.VMEM((SUBLANES_F32, tm + 2 * HALO - SUBLANES_F32, CONF_LANE_CHUNK),
                                   F32)],
        compiler_params=_params(("parallel",)),
        name="conformer_inner",
    )(h, h, h, w1, b1.reshape(1, 2 * d), dw_w, dw_b.reshape(1, d),
      ln_g.reshape(1, d), ln_b.reshape(1, d))


def _run_trunk(x, p):
    batch, seq_len, d = x.shape
    t = batch * seq_len
    xf = x.reshape(t, d)
    rope_t = _rope_tables_t(seq_len)
    tabs = _dft_tables(seq_len)
    h = rmsnorm_bf16(xf, p["norm_mix_pre"][0])
    for i in range(DEPTH):
        j, kind = i // N_MIXERS, i % N_MIXERS
        g_post, g_ffn = p["norm_mix_post"][i], p["norm_ffn_pre"][i]
        if kind == 0:
            q, k, v = qkv_project(h, p["attn_w_qkv_t"], j, p["attn_q_gain"][j],
                                  p["attn_k_gain"][j], rope_t, seq_len)
            bound = score_bound(p["attn_q_gain"][j], p["attn_k_gain"][j])
            o = flash_attention(q, k, v, bound, batch, seq_len)
            xf, h = proj_residual(o, p["attn_w_o"], j, None, xf, g_post, g_ffn)
        elif kind == 1:
            y = fourier_real_2d(h, batch, seq_len, tabs)
            xf, h = proj_residual(y, p["fnet_w_out"], j, p["fnet_b_out"][j], xf, g_post, g_ffn)
        else:
            u = conformer_inner(h, p["conv_w_pw1"], j, p["conv_b_pw1"][j], p["conv_dw_w"][j],
                                p["conv_dw_b"][j], p["conv_ln_g"][j], p["conv_ln_b"][j], seq_len)
            xf, h = proj_residual(u, p["conv_w_pw2"], j, p["conv_b_pw2"][j], xf, g_post, g_ffn)
        g_next = p["norm_mix_pre"][i + 1] if i + 1 < DEPTH else None
        xf, h = conv_ffn_residual(h, p["ffn_w_up_c"], p["ffn_dw_w"], p["ffn_dw_b"],
                                  p["ffn_w_down"], i, xf, p["norm_ffn_post"][i], g_next, seq_len)
    return xf.reshape(batch, seq_len, d)


def kernel(x_prompt, x_sample, norm_mix_pre, norm_mix_post, norm_ffn_pre, norm_ffn_post, attn_w_qkv, attn_q_gain, attn_k_gain, attn_w_o, fnet_w_out, fnet_b_out, conv_w_pw1, conv_b_pw1, conv_dw_w, conv_dw_b, conv_ln_g, conv_ln_b, conv_w_pw2, conv_b_pw2, ffn_w_up, ffn_dw_w, ffn_dw_b, ffn_w_down):
    p = {
        "norm_mix_pre": norm_mix_pre, "norm_mix_post": norm_mix_post,
        "norm_ffn_pre": norm_ffn_pre, "norm_ffn_post": norm_ffn_post,
        "attn_w_qkv_t": jnp.swapaxes(attn_w_qkv, 1, 2).astype(BF16),
        "attn_q_gain": attn_q_gain, "attn_k_gain": attn_k_gain,
        "attn_w_o": attn_w_o.astype(BF16),
        "fnet_w_out": fnet_w_out.astype(BF16), "fnet_b_out": fnet_b_out,
        "conv_w_pw1": conv_w_pw1.astype(BF16), "conv_b_pw1": conv_b_pw1,
        "conv_dw_w": conv_dw_w, "conv_dw_b": conv_dw_b,
        "conv_ln_g": conv_ln_g, "conv_ln_b": conv_ln_b,
        "conv_w_pw2": conv_w_pw2.astype(BF16), "conv_b_pw2": conv_b_pw2,
        "ffn_w_up_c": ffn_up_chunked(ffn_w_up), "ffn_dw_w": ffn_dw_w, "ffn_dw_b": ffn_dw_b,
        "ffn_w_down": ffn_w_down.astype(BF16),
    }
    return (_run_trunk(x_prompt, p), _run_trunk(x_sample, p))
```

```python
import functools
import math

import numpy as np
import jax
import jax.numpy as jnp
from jax import lax
from jax.experimental import pallas as pl
from jax.experimental.pallas import tpu as pltpu

F32 = jnp.float32
BF16 = jnp.bfloat16

D_MODEL = 2048
DEPTH = 4
N_MIXERS = 3
GRID_W = 64
HEAD_DIM = 128
N_HEADS = D_MODEL // HEAD_DIM
N_KV_HEADS = N_HEADS // 4
GQA_GROUP = N_HEADS // N_KV_HEADS
ROPE_THETA = 10000.0
ROPE_AXIS_DIM = HEAD_DIM // 2
ROPE_F = ROPE_AXIS_DIM // 2
FNET_GROUPS = 4
FNET_GROUP_DIM = D_MODEL // FNET_GROUPS
CONF_KERNEL = 31
CONF_PAD = (CONF_KERNEL - 1) // 2
FFN_DIM = 4 * D_MODEL
EPS = 1e-6

V7X_VMEM_BYTES = 64 * 1024 * 1024
V7X_VMEM_LIMIT = V7X_VMEM_BYTES - 8 * 1024 * 1024
LANES = 128
SUBLANES_F32 = 8
SUBLANES_BF16 = 16
HALO = SUBLANES_BF16

ROW_TILE = 512
FFN_CHUNK = 512
FFN_SUBCHUNK = 256
FFN_ROW_BLOCKS = 2
CONF_ROW_TILE = 256
CONF_LANE_CHUNK = 512
CONF_ROW_BLOCK = 32
ATTN_TQ = 512
ATTN_TK = 512
ATTN_SLOTS = 4
ATTN_MAX_FIXED_SHIFT = 60.0
ATTN_BOUND_SLACK = 1.02
DFT_INTERLEAVE = SUBLANES_F32


def _params(semantics):
    return pltpu.CompilerParams(dimension_semantics=semantics,
                                vmem_limit_bytes=V7X_VMEM_LIMIT)


def _resident(shape, index_map):
    return pl.BlockSpec(shape, index_map, pipeline_mode=pl.Buffered(1))


def _rms(x, g):
    return x * lax.rsqrt(jnp.mean(x * x, axis=-1, keepdims=True) + EPS) * g


def _sigmoid(x):
    return 1.0 / (1.0 + jnp.exp(-x))


def _gelu_tanh(x):
    a = -2.0 * math.sqrt(2.0 / math.pi) * math.log2(math.e)
    return x / (1.0 + jnp.exp2(x * (a + (a * 0.044715) * (x * x))))


def _dot(a, b):
    return jnp.dot(a, b, preferred_element_type=F32)


def _store_sandwich(m, x_ref, gp_ref, gn_ref, o_ref, h_ref):
    xn = x_ref[...] + _rms(m, gp_ref[...])
    o_ref[...] = xn
    if h_ref is not None:
        h_ref[...] = _rms(xn, gn_ref[...]).astype(BF16)


def _rmsnorm_kernel(x_ref, g_ref, o_ref):
    o_ref[...] = _rms(x_ref[...], g_ref[...]).astype(o_ref.dtype)


def rmsnorm_bf16(x, g):
    t, d = x.shape
    return pl.pallas_call(
        _rmsnorm_kernel,
        grid=(t // ROW_TILE,),
        in_specs=[pl.BlockSpec((ROW_TILE, d), lambda i: (i, 0)),
                  pl.BlockSpec((1, d), lambda i: (0, 0))],
        out_specs=pl.BlockSpec((ROW_TILE, d), lambda i: (i, 0)),
        out_shape=jax.ShapeDtypeStruct((t, d), BF16),
        compiler_params=_params(("parallel",)),
        name="rmsnorm",
    )(x, g.reshape(1, d))


def _proj_res_kernel(*refs, has_bias, has_next):
    a_ref, w_ref = refs[0], refs[1]
    pos = 2
    b_ref = None
    if has_bias:
        b_ref = refs[pos]
        pos += 1
    x_ref, gp_ref = refs[pos], refs[pos + 1]
    pos += 2
    gn_ref = None
    if has_next:
        gn_ref = refs[pos]
        pos += 1
    o_ref = refs[pos]
    h_ref = refs[pos + 1] if has_next else None
    m = _dot(a_ref[...].astype(BF16), w_ref[...])
    if has_bias:
        m = m + b_ref[...]
    _store_sandwich(m, x_ref, gp_ref, gn_ref, o_ref, h_ref)


def proj_residual(a, w, layer, bias, x, g_post, g_next):
    t, k = a.shape
    d = w.shape[2]
    has_bias = bias is not None
    has_next = g_next is not None
    row = lambda i: (i, 0)
    fixed = lambda i: (0, 0)
    in_specs = [pl.BlockSpec((ROW_TILE, k), row),
                _resident((None, k, d), lambda i: (layer, 0, 0))]
    args = [a, w]
    if has_bias:
        in_specs.append(pl.BlockSpec((1, d), fixed))
        args.append(bias.reshape(1, d))
    in_specs += [pl.BlockSpec((ROW_TILE, d), row), pl.BlockSpec((1, d), fixed)]
    args += [x, g_post.reshape(1, d)]
    out_specs = [pl.BlockSpec((ROW_TILE, d), row)]
    out_shape = [jax.ShapeDtypeStruct((t, d), F32)]
    if has_next:
        in_specs.append(pl.BlockSpec((1, d), fixed))
        args.append(g_next.reshape(1, d))
        out_specs.append(pl.BlockSpec((ROW_TILE, d), row))
        out_shape.append(jax.ShapeDtypeStruct((t, d), BF16))
    outs = pl.pallas_call(
        functools.partial(_proj_res_kernel, has_bias=has_bias, has_next=has_next),
        grid=(t // ROW_TILE,),
        in_specs=in_specs, out_specs=out_specs, out_shape=out_shape,
        compiler_params=_params(("parallel",)),
        name="proj_residual",
    )(*args)
    return (outs[0], outs[1]) if has_next else (outs[0], None)


def _halo_specs(tm, d, n_tiles):
    per = tm // HALO
    last = n_tiles * per - 1
    return [pl.BlockSpec((HALO, d), lambda i, *_: (jnp.maximum(i * per - 1, 0), 0)),
            pl.BlockSpec((tm, d), lambda i, *_: (i, 0)),
            pl.BlockSpec((HALO, d), lambda i, *_: (jnp.minimum((i + 1) * per, last), 0))]


def _seq_edges(i, tiles_per_seq):
    pos = i % tiles_per_seq
    return pos == 0, pos == tiles_per_seq - 1


def _ffn_kernel(*refs, tm, tc, tiles_per_seq, n_chunks, has_next):
    (hp_ref, hm_ref, hn_ref, wg_ref, wv_ref, cg_ref, cv_ref, bg_ref, bv_ref,
     wd_ref, x_ref, gp_ref) = refs[:12]
    pos = 12
    gn_ref = None
    if has_next:
        gn_ref = refs[pos]
        pos += 1
    o_ref = refs[pos]
    pos += 1
    h_ref = None
    if has_next:
        h_ref = refs[pos]
        pos += 1
    hh_ref, acc_ref = refs[pos], refs[pos + 1]

    i = pl.program_id(0)
    c = pl.program_id(1)

    @pl.when(c == 0)
    def _():
        first, last = _seq_edges(i, tiles_per_seq)
        keep_prev = jnp.where(first, 0.0, 1.0).astype(BF16)
        keep_next = jnp.where(last, 0.0, 1.0).astype(BF16)
        hh_ref[0:HALO, :] = hp_ref[...] * keep_prev
        hh_ref[HALO:HALO + tm, :] = hm_ref[...]
        hh_ref[HALO + tm:, :] = hn_ref[...] * keep_next
        acc_ref[...] = jnp.zeros_like(acc_ref)

    rows = (tm + 2 * HALO) // FFN_ROW_BLOCKS

    def up_conv(w_ref, c_ref, b_ref, sl):
        v = jnp.concatenate([_dot(hh_ref[r * rows:(r + 1) * rows, :], w_ref[:, sl])
                             for r in range(FFN_ROW_BLOCKS)], axis=0)
        w = c_ref[:, sl]
        return (w[0:1, :] * v[HALO - 1:HALO - 1 + tm] + w[1:2, :] * v[HALO:HALO + tm]
                + w[2:3, :] * v[HALO + 1:HALO + 1 + tm] + b_ref[:, sl])

    subs = [slice(s * FFN_SUBCHUNK, (s + 1) * FFN_SUBCHUNK) for s in range(tc // FFN_SUBCHUNK)]
    acts = []
    for sl in subs:
        ug = up_conv(wg_ref, cg_ref, bg_ref, sl)
        uv = up_conv(wv_ref, cv_ref, bv_ref, sl)
        acts.append((_gelu_tanh(ug) * uv).astype(BF16))
    for sl, act in zip(subs, acts):
        acc_ref[...] += _dot(act, wd_ref[sl, :])

    @pl.when(c == n_chunks - 1)
    def _():
        _store_sandwich(acc_ref[...], x_ref, gp_ref, gn_ref, o_ref, h_ref)


def conv_ffn_residual(h, w_up, dw_w, dw_b, w_down, layer, x, g_post, g_next, seq_len):
    t, d = h.shape
    f = w_down.shape[1]
    tm, tc = ROW_TILE, FFN_CHUNK
    n_tiles, n_chunks = t // tm, f // tc
    has_next = g_next is not None
    row = lambda i, c: (i, 0)
    fixed = lambda i, c: (0, 0)
    in_specs = _halo_specs(tm, d, n_tiles) + [
        pl.BlockSpec((None, d, tc), lambda i, c: (layer, 0, c)),
        pl.BlockSpec((None, d, tc), lambda i, c: (layer, 0, c + n_chunks)),
        pl.BlockSpec((None, 3, tc), lambda i, c: (layer, 0, c)),
        pl.BlockSpec((None, 3, tc), lambda i, c: (layer, 0, c + n_chunks)),
        pl.BlockSpec((None, 1, tc), lambda i, c: (layer, 0, c)),
        pl.BlockSpec((None, 1, tc), lambda i, c: (layer, 0, c + n_chunks)),
        pl.BlockSpec((None, tc, d), lambda i, c: (layer, c, 0)),
        pl.BlockSpec((tm, d), row),
        pl.BlockSpec((1, d), fixed),
    ]
    dw_b3 = dw_b.reshape(dw_b.shape[0], 1, 2 * f)
    args = [h, h, h, w_up, w_up, dw_w, dw_w, dw_b3, dw_b3, w_down, x, g_post.reshape(1, d)]
    out_specs = [pl.BlockSpec((tm, d), row)]
    out_shape = [jax.ShapeDtypeStruct((t, d), F32)]
    if has_next:
        in_specs.append(pl.BlockSpec((1, d), fixed))
        args.append(g_next.reshape(1, d))
        out_specs.append(pl.BlockSpec((tm, d), row))
        out_shape.append(jax.ShapeDtypeStruct((t, d), BF16))
    outs = pl.pallas_call(
        functools.partial(_ffn_kernel, tm=tm, tc=tc, tiles_per_seq=seq_len // tm,
                          n_chunks=n_chunks, has_next=has_next),
        grid=(n_tiles, n_chunks),
        in_specs=in_specs, out_specs=out_specs, out_shape=out_shape,
        scratch_shapes=[pltpu.VMEM((tm + 2 * HALO, d), BF16), pltpu.VMEM((tm, d), F32)],
        compiler_params=_params(("parallel", "arbitrary")),
        name="conv_ffn",
    )(*args)
    return (outs[0], outs[1]) if has_next else (outs[0], None)


def _head_norm_rope(x, gain, cr, sr, cc, sc):
    y = x * lax.rsqrt(jnp.mean(x * x, axis=0, keepdims=True) + EPS) * gain
    f = ROPE_F
    r1, r2, c1, c2 = y[0:f], y[f:2 * f], y[2 * f:3 * f], y[3 * f:4 * f]
    return jnp.concatenate([r1 * cr - r2 * sr, r2 * cr + r1 * sr,
                            c1 * cc - c2 * sc, c2 * cc + c1 * sc], axis=0)


def _qkv_kernel(h_ref, wt_ref, gq_ref, gk_ref, cr_ref, sr_ref, cc_ref, sc_ref,
                q_ref, k_ref, v_ref, *, tn, q_scale):
    h = h_ref[...]
    rope = (cr_ref[...], sr_ref[...], cc_ref[...], sc_ref[...])
    grp = GQA_GROUP * HEAD_DIM
    nt_dims = (((1,), (1,)), ((), ()))

    def proj(row0):
        return lax.dot_general(wt_ref[row0:row0 + grp, :], h, nt_dims,
                               preferred_element_type=F32)

    gq = gq_ref[...]
    for kv in range(N_KV_HEADS):
        r = proj(kv * grp)
        for g in range(GQA_GROUP):
            y = _head_norm_rope(r[g * HEAD_DIM:(g + 1) * HEAD_DIM], gq, *rope) * q_scale
            y = y.astype(BF16)
            for j in range(tn // ATTN_TQ):
                col = (j * GQA_GROUP + g) * ATTN_TQ
                q_ref[kv * HEAD_DIM:(kv + 1) * HEAD_DIM, col:col + ATTN_TQ] = (
                    y[:, j * ATTN_TQ:(j + 1) * ATTN_TQ])
    gk = gk_ref[...]
    r = proj(N_HEADS * HEAD_DIM)
    for kv in range(N_KV_HEADS):
        y = _head_norm_rope(r[kv * HEAD_DIM:(kv + 1) * HEAD_DIM], gk, *rope)
        k_ref[:, kv * HEAD_DIM:(kv + 1) * HEAD_DIM] = y.T.astype(BF16)
    r = proj((N_HEADS + N_KV_HEADS) * HEAD_DIM)
    ones = jnp.ones((SUBLANES_BF16, tn), BF16)
    for kv in range(N_KV_HEADS):
        v_ref[kv, 0, 0:HEAD_DIM, :] = r[kv * HEAD_DIM:(kv + 1) * HEAD_DIM].astype(BF16)
        v_ref[kv, 0, HEAD_DIM:, :] = ones


def qkv_project(h, w_t, layer, q_gain, k_gain, rope_t, seq_len):
    t, d = h.shape
    tn = ATTN_TK
    per_seq = seq_len // tn
    q_scale = HEAD_DIM ** -0.5 * math.log2(math.e)
    gq = jnp.broadcast_to(q_gain.reshape(HEAD_DIM, 1), (HEAD_DIM, tn))
    gk = jnp.broadcast_to(k_gain.reshape(HEAD_DIM, 1), (HEAD_DIM, tn))
    rope_spec = pl.BlockSpec((ROPE_F, tn), lambda i: (0, i % per_seq))
    gain_spec = pl.BlockSpec((HEAD_DIM, tn), lambda i: (0, 0))
    return pl.pallas_call(
        functools.partial(_qkv_kernel, tn=tn, q_scale=q_scale),
        grid=(t // tn,),
        in_specs=[pl.BlockSpec((tn, d), lambda i: (i, 0)),
                  _resident((None,) + w_t.shape[1:], lambda i: (layer, 0, 0)),
                  gain_spec, gain_spec, rope_spec, rope_spec, rope_spec, rope_spec],
        out_specs=[pl.BlockSpec((N_KV_HEADS * HEAD_DIM, GQA_GROUP * tn), lambda i: (0, i)),
                   pl.BlockSpec((tn, N_KV_HEADS * HEAD_DIM), lambda i: (i, 0)),
                   pl.BlockSpec((N_KV_HEADS, 1, HEAD_DIM + SUBLANES_BF16, tn),
                                lambda i: (0, i, 0, 0))],
        out_shape=[jax.ShapeDtypeStruct((N_KV_HEADS * HEAD_DIM, GQA_GROUP * t), BF16),
                   jax.ShapeDtypeStruct((t, N_KV_HEADS * HEAD_DIM), BF16),
                   jax.ShapeDtypeStruct((N_KV_HEADS, t // tn, HEAD_DIM + SUBLANES_BF16, tn),
                                        BF16)],
        compiler_params=_params(("parallel",)),
        name="qkv_project",
    )(h, w_t, gq, gk, *rope_t)


def _attn_online(q, k_ref, v_ref, m_ref, acc_ref, s_ref, mb_ref, p_ref, al_ref, n_kblocks):
    last_slot = ATTN_SLOTS - 1

    def scores(start, slot):
        s = _dot(k_ref[pl.ds(start, ATTN_TK), :], q)
        s_ref[slot] = s
        mb_ref[slot] = jnp.max(s, axis=0, keepdims=True)

    m_ref[...] = jnp.full_like(m_ref, -jnp.inf)
    acc_ref[...] = jnp.zeros_like(acc_ref)
    p_ref[last_slot] = jnp.zeros(p_ref.shape[1:], p_ref.dtype)
    al_ref[last_slot] = jnp.ones(al_ref.shape[1:], al_ref.dtype)
    scores(0, 0)

    def step(c, slot):
        nxt, prv = (slot + 1) % ATTN_SLOTS, (slot - 1) % ATTN_SLOTS
        c_next = jnp.minimum(c + 1, n_kblocks - 1)
        scores(pl.multiple_of(c_next * ATTN_TK, ATTN_TK), nxt)
        c_prev = jnp.maximum(c - 1, 0)
        acc_ref[...] = acc_ref[...] * al_ref[prv] + _dot(v_ref[0, c_prev], p_ref[prv])
        m_old = m_ref[...]
        m_new = jnp.maximum(m_old, mb_ref[slot])
        p_ref[slot] = jnp.exp2(s_ref[slot] - m_new).astype(p_ref.dtype)
        al_ref[slot] = jnp.exp2(m_old - m_new)
        m_ref[...] = m_new

    def body(i, carry):
        for j in range(ATTN_SLOTS):
            step(ATTN_SLOTS * i + j, j)
        return carry

    lax.fori_loop(0, n_kblocks // ATTN_SLOTS, body, 0)
    acc_ref[...] = (acc_ref[...] * al_ref[last_slot]
                    + _dot(v_ref[0, n_kblocks - 1], p_ref[last_slot]))


def _attn_fixed_shift(q, k_ref, v_ref, acc_ref, shift, n_kblocks):
    acc_ref[...] = jnp.zeros_like(acc_ref)

    def body(i, carry):
        acc = acc_ref[...]
        for j in range(ATTN_SLOTS):
            c = ATTN_SLOTS * i + j
            start = pl.multiple_of(c * ATTN_TK, ATTN_TK)
            s = _dot(k_ref[pl.ds(start, ATTN_TK), :], q)
            acc = acc + _dot(v_ref[0, c], jnp.exp2(s - shift).astype(BF16))
        acc_ref[...] = acc
        return carry

    lax.fori_loop(0, n_kblocks // ATTN_SLOTS, body, 0)


def _attn_kernel(bound_ref, q_ref, k_ref, v_ref, o_ref, m_ref, acc_ref, s_ref, mb_ref, p_ref,
                 al_ref, *, n_kblocks):
    q = q_ref[...]
    bound = bound_ref[0, 0]

    small = bound <= ATTN_MAX_FIXED_SHIFT

    @pl.when(small)
    def _():
        _attn_fixed_shift(q, k_ref, v_ref, acc_ref, bound, n_kblocks)

    @pl.when(jnp.logical_not(small))
    def _():
        _attn_online(q, k_ref, v_ref, m_ref, acc_ref, s_ref, mb_ref, p_ref, al_ref, n_kblocks)

    acc = acc_ref[...]
    out = acc[0:HEAD_DIM] / acc[HEAD_DIM:HEAD_DIM + 1]
    for g in range(GQA_GROUP):
        o_ref[:, g * HEAD_DIM:(g + 1) * HEAD_DIM] = (
            out[:, g * ATTN_TQ:(g + 1) * ATTN_TQ].T.astype(o_ref.dtype))


def score_bound(q_gain, k_gain):
    b = (HEAD_DIM ** 0.5 * math.log2(math.e)) * jnp.max(jnp.abs(q_gain)) * jnp.max(jnp.abs(k_gain))
    return (b * ATTN_BOUND_SLACK).reshape(1, 1).astype(F32)


def flash_attention(q, k, v, bound, batch, seq_len):
    t = batch * seq_len
    q_per_seq = seq_len // ATTN_TQ
    n_kblocks = seq_len // ATTN_TK
    assert n_kblocks % ATTN_SLOTS == 0
    lanes = GQA_GROUP * ATTN_TQ
    return pl.pallas_call(
        functools.partial(_attn_kernel, n_kblocks=n_kblocks),
        grid=(batch, N_KV_HEADS, q_per_seq),
        in_specs=[pl.BlockSpec(memory_space=pltpu.SMEM),
                  pl.BlockSpec((HEAD_DIM, lanes), lambda b, kv, qi: (kv, b * q_per_seq + qi)),
                  pl.BlockSpec((seq_len, HEAD_DIM), lambda b, kv, qi: (b, kv)),
                  pl.BlockSpec((1, n_kblocks, HEAD_DIM + SUBLANES_BF16, ATTN_TK),
                               lambda b, kv, qi: (kv, b, 0, 0))],
        out_specs=pl.BlockSpec((ATTN_TQ, GQA_GROUP * HEAD_DIM),
                               lambda b, kv, qi: (b * q_per_seq + qi, kv)),
        out_shape=jax.ShapeDtypeStruct((t, N_HEADS * HEAD_DIM), BF16),
        scratch_shapes=[pltpu.VMEM((1, lanes), F32),
                        pltpu.VMEM((HEAD_DIM + SUBLANES_BF16, lanes), F32),
                        pltpu.VMEM((ATTN_SLOTS, ATTN_TK, lanes), F32),
                        pltpu.VMEM((ATTN_SLOTS, 1, lanes), F32),
                        pltpu.VMEM((ATTN_SLOTS, ATTN_TK, lanes), BF16),
                        pltpu.VMEM((ATTN_SLOTS, 1, lanes), F32)],
        compiler_params=_params(("parallel", "parallel", "parallel")),
        name="flash_attention",
    )(bound, q, k, v)


def _rope_tables_t(n):
    rows = n // GRID_W
    row = jnp.repeat(jnp.arange(rows, dtype=F32), GRID_W)
    col = jnp.tile(jnp.arange(GRID_W, dtype=F32), rows)
    inv_freq = ROPE_THETA ** (-jnp.arange(0, ROPE_AXIS_DIM, 2, dtype=F32) / ROPE_AXIS_DIM)
    ang_r = inv_freq[:, None] * row[None, :]
    ang_c = inv_freq[:, None] * col[None, :]
    return jnp.cos(ang_r), jnp.sin(ang_r), jnp.cos(ang_c), jnp.sin(ang_c)


def _dft_split(n):
    n2 = max(1, int(round(math.sqrt(n) / 4)))
    while n % n2:
        n2 -= 1
    return n // n2, n2


def _dft_tables(n):
    n1, n2 = _dft_split(n)
    k1 = np.arange(n1)
    ang1 = 2.0 * np.pi * ((k1[:, None] * k1[None, :]) % n1) / n1
    m1 = np.concatenate([np.cos(ang1), -np.sin(ang1)], axis=0)
    t2 = np.arange(n2)
    ang_tw = 2.0 * np.pi * (t2[:, None] * k1[None, :]) / n
    ang2 = 2.0 * np.pi * ((t2[:, None] * t2[None, :]) % n2) / n2
    eye = np.eye(DFT_INTERLEAVE)
    kc = np.einsum("kt,ij->kijt", np.cos(ang2), eye).reshape(n2 * DFT_INTERLEAVE, -1)
    ks = np.einsum("kt,ij->kijt", np.sin(ang2), eye).reshape(n2 * DFT_INTERLEAVE, -1)
    k_re = np.concatenate([kc, ks], axis=1)
    k_im = np.concatenate([-ks, kc], axis=1)
    c = np.arange(FNET_GROUP_DIM)
    ang_c = 2.0 * np.pi * ((c[:, None] * c[None, :]) % FNET_GROUP_DIM) / FNET_GROUP_DIM
    as_bf16 = lambda a: jnp.asarray(a, F32).astype(BF16)
    tw_shape = (n2, n1, LANES)
    return dict(
        n1=n1, n2=n2, m1=as_bf16(m1), k_re=as_bf16(k_re), k_im=as_bf16(k_im),
        cc=as_bf16(np.cos(ang_c)), sc=as_bf16(np.sin(ang_c)),
        tw_cos=jnp.broadcast_to(jnp.asarray(np.cos(ang_tw), F32)[:, :, None], tw_shape),
        tw_sin=jnp.broadcast_to(jnp.asarray(np.sin(ang_tw), F32)[:, :, None], tw_shape))


def _dft1_kernel(x_ref, m1_ref, twc_ref, tws_ref, o_ref, *, n1):
    r = _dot(m1_ref[...], x_ref[...])
    twc, tws = twc_ref[0], tws_ref[0]
    for j in range(x_ref.shape[1] // LANES):
        sl = slice(j * LANES, (j + 1) * LANES)
        ar, ai = r[0:n1, sl], r[n1:, sl]
        o_ref[0, 0, :, sl] = (ar * twc + ai * tws).astype(o_ref.dtype)
        o_ref[0, 1, :, sl] = (ai * twc - ar * tws).astype(o_ref.dtype)


def _dft2_kernel(a_ref, kre_ref, kim_ref, cc_ref, sc_ref, o_ref, *, rows, n2, scale):
    a = a_ref[0].reshape(2 * rows, a_ref.shape[-1])
    zr = _dot(kre_ref[...], a).astype(BF16)
    zi = _dot(kim_ref[...], a).astype(BF16)
    cc, sc = cc_ref[...], sc_ref[...]
    for g in range(FNET_GROUPS):
        sl = slice(g * FNET_GROUP_DIM, (g + 1) * FNET_GROUP_DIM)
        y = (_dot(zr[:, sl], cc) + _dot(zi[:, sl], sc)) * scale
        o_ref[0, :, :, sl] = y.reshape(n2, DFT_INTERLEAVE, FNET_GROUP_DIM)


def fourier_real_2d(h, batch, seq_len, tabs):
    t, d = h.shape
    n1, n2 = tabs["n1"], tabs["n2"]
    rows = n2 * DFT_INTERLEAVE
    a2 = pl.pallas_call(
        functools.partial(_dft1_kernel, n1=n1),
        grid=(batch, n2),
        in_specs=[pl.BlockSpec((n1, d), lambda b, j: (b, j)),
                  _resident((2 * n1, n1), lambda b, j: (0, 0)),
                  pl.BlockSpec((1, n1, LANES), lambda b, j: (j, 0, 0)),
                  pl.BlockSpec((1, n1, LANES), lambda b, j: (j, 0, 0))],
        out_specs=pl.BlockSpec((1, 2, n1, d), lambda b, j: (b, 0, 0, j)),
        out_shape=jax.ShapeDtypeStruct((batch, 2, n1, n2 * d), BF16),
        compiler_params=_params(("parallel", "parallel")),
        name="dft_stage1",
    )(h.reshape(batch * n1, n2 * d), tabs["m1"], tabs["tw_cos"], tabs["tw_sin"])
    scale = 1.0 / math.sqrt(seq_len * FNET_GROUP_DIM)
    y = pl.pallas_call(
        functools.partial(_dft2_kernel, rows=rows, n2=n2, scale=scale),
        grid=(batch, n1 // DFT_INTERLEAVE),
        in_specs=[pl.BlockSpec((1, 2, rows, d), lambda b, j: (b, 0, j, 0)),
                  _resident((rows, 2 * rows), lambda b, j: (0, 0)),
                  _resident((rows, 2 * rows), lambda b, j: (0, 0)),
                  _resident((FNET_GROUP_DIM, FNET_GROUP_DIM), lambda b, j: (0, 0)),
                  _resident((FNET_GROUP_DIM, FNET_GROUP_DIM), lambda b, j: (0, 0))],
        out_specs=pl.BlockSpec((1, n2, DFT_INTERLEAVE, d), lambda b, j: (b, 0, j, 0)),
        out_shape=jax.ShapeDtypeStruct((batch, n2, n1, d), F32),
        compiler_params=_params(("parallel", "parallel")),
        name="dft_stage2",
    )(a2.reshape(batch, 2, n1 * n2, d), tabs["k_re"], tabs["k_im"], tabs["cc"], tabs["sc"])
    return y.reshape(t, d)


def _conf_kernel(hp_ref, hm_ref, hn_ref, w1_ref, b1_ref, dw_ref, db_ref, lg_ref, lb_ref,
                 o_ref, hh_ref, y_ref, sh_ref, *, tm, tiles_per_seq):
    d = o_ref.shape[1]
    i = pl.program_id(0)
    first, last = _seq_edges(i, tiles_per_seq)
    hh_ref[0:HALO, :] = hp_ref[...]
    hh_ref[HALO:HALO + tm, :] = hm_ref[...]
    hh_ref[HALO + tm:, :] = hn_ref[...]
    a = hh_ref[...]
    ext = tm + 2 * HALO
    rows = lax.broadcasted_iota(jnp.int32, (ext, 1), 0)
    lo = jnp.where(first, HALO, 0)
    hi = jnp.where(last, HALO + tm, ext)
    inside = (rows >= lo) & (rows < hi)
    span = tm + SUBLANES_F32 * (2 * HALO // SUBLANES_F32 - 1)
    for j in range(d // CONF_LANE_CHUNK):
        sl = slice(j * CONF_LANE_CHUNK, (j + 1) * CONF_LANE_CHUNK)
        sg = slice(d + j * CONF_LANE_CHUNK, d + (j + 1) * CONF_LANE_CHUNK)
        za = _dot(a, w1_ref[:, sl]) + b1_ref[:, sl]
        zg = _dot(a, w1_ref[:, sg]) + b1_ref[:, sg]
        u = jnp.where(inside, za * _sigmoid(zg), 0.0)
        for b in range(SUBLANES_F32):
            sh_ref[b] = u[b:b + span]
        sub = CONF_ROW_BLOCK // SUBLANES_F32
        for cc in range(CONF_LANE_CHUNK // LANES):
            lanes = slice(cc * LANES, (cc + 1) * LANES)
            col = slice(j * CONF_LANE_CHUNK + cc * LANES, j * CONF_LANE_CHUNK + (cc + 1) * LANES)
            taps = [jnp.broadcast_to(dw_ref[k:k + 1, col], (SUBLANES_F32, LANES))
                    for k in range(CONF_KERNEL)]
            bias = jnp.broadcast_to(db_ref[:, col], (SUBLANES_F32, LANES))
            for r0 in range(0, tm, CONF_ROW_BLOCK):
                acc = jnp.broadcast_to(bias[None], (sub, SUBLANES_F32, LANES))
                for k in range(CONF_KERNEL):
                    off = HALO - CONF_PAD + k
                    base = (off // SUBLANES_F32) * SUBLANES_F32 + r0
                    win = sh_ref[off % SUBLANES_F32, base:base + CONF_ROW_BLOCK, lanes]
                    acc = acc + taps[k][None] * win.reshape(sub, SUBLANES_F32, LANES)
                y_ref[r0:r0 + CONF_ROW_BLOCK, col] = acc.reshape(CONF_ROW_BLOCK, LANES)
    y = y_ref[...]
    mu = jnp.mean(y, axis=-1, keepdims=True)
    yc = y - mu
    yn = yc * lax.rsqrt(jnp.mean(yc * yc, axis=-1, keepdims=True) + EPS) * lg_ref[...] + lb_ref[...]
    o_ref[...] = (yn * _sigmoid(yn)).astype(o_ref.dtype)


def conformer_inner(h, w1, layer, b1, dw_w, dw_b, ln_g, ln_b, seq_len):
    t, d = h.shape
    tm = CONF_ROW_TILE
    n_tiles = t // tm
    fixed = lambda i: (0, 0)
    return pl.pallas_call(
        functools.partial(_conf_kernel, tm=tm, tiles_per_seq=seq_len // tm),
        grid=(n_tiles,),
        in_specs=_halo_specs(tm, d, n_tiles) + [
            _resident((None, d, 2 * d), lambda i: (layer, 0, 0)), pl.BlockSpec((1, 2 * d), fixed),
            pl.BlockSpec((CONF_KERNEL, d), fixed), pl.BlockSpec((1, d), fixed),
            pl.BlockSpec((1, d), fixed), pl.BlockSpec((1, d), fixed)],
        out_specs=pl.BlockSpec((tm, d), lambda i: (i, 0)),
        out_shape=jax.ShapeDtypeStruct((t, d), BF16),
        scratch_shapes=[pltpu.VMEM((tm + 2 * HALO, d), BF16), pltpu.VMEM((tm, d), F32),
                        pltpu.VMEM((SUBLANES_F32, tm + 2 * HALO - SUBLANES_F32, CONF_LANE_CHUNK),
                                   F32)],
        compiler_params=_params(("parallel",)),
        name="conformer_inner",
    )(h, h, h, w1, b1.reshape(1, 2 * d), dw_w, dw_b.reshape(1, d),
      ln_g.reshape(1, d), ln_b.reshape(1, d))


def _run_trunk(x, p):
    batch, seq_len, d = x.shape
    t = batch * seq_len
    xf = x.reshape(t, d)
    rope_t = _rope_tables_t(seq_len)
    tabs = _dft_tables(seq_len)
    h = rmsnorm_bf16(xf, p["norm_mix_pre"][0])
    for i in range(DEPTH):
        j, kind = i // N_MIXERS, i % N_MIXERS
        g_post, g_ffn = p["norm_mix_post"][i], p["norm_ffn_pre"][i]
        if kind == 0:
            q, k, v = qkv_project(h, p["attn_w_qkv_t"], j, p["attn_q_gain"][j],
                                  p["attn_k_gain"][j], rope_t, seq_len)
            bound = score_bound(p["attn_q_gain"][j], p["attn_k_gain"][j])
            o = flash_attention(q, k, v, bound, batch, seq_len)
            xf, h = proj_residual(o, p["attn_w_o"], j, None, xf, g_post, g_ffn)
        elif kind == 1:
            y = fourier_real_2d(h, batch, seq_len, tabs)
            xf, h = proj_residual(y, p["fnet_w_out"], j, p["fnet_b_out"][j], xf, g_post, g_ffn)
        else:
            u = conformer_inner(h, p["conv_w_pw1"], j, p["conv_b_pw1"][j], p["conv_dw_w"][j],
                                p["conv_dw_b"][j], p["conv_ln_g"][j], p["conv_ln_b"][j], seq_len)
            xf, h = proj_residual(u, p["conv_w_pw2"], j, p["conv_b_pw2"][j], xf, g_post, g_ffn)
        g_next = p["norm_mix_pre"][i + 1] if i + 1 < DEPTH else None
        xf, h = conv_ffn_residual(h, p["ffn_w_up"], p["ffn_dw_w"], p["ffn_dw_b"],
                                  p["ffn_w_down"], i, xf, p["norm_ffn_post"][i], g_next, seq_len)
    return xf.reshape(batch, seq_len, d)


def kernel(x_prompt, x_sample, norm_mix_pre, norm_mix_post, norm_ffn_pre, norm_ffn_post, attn_w_qkv, attn_q_gain, attn_k_gain, attn_w_o, fnet_w_out, fnet_b_out, conv_w_pw1, conv_b_pw1, conv_dw_w, conv_dw_b, conv_ln_g, conv_ln_b, conv_w_pw2, conv_b_pw2, ffn_w_up, ffn_dw_w, ffn_dw_b, ffn_w_down):
    p = {
        "norm_mix_pre": norm_mix_pre, "norm_mix_post": norm_mix_post,
        "norm_ffn_pre": norm_ffn_pre, "norm_ffn_post": norm_ffn_post,
        "attn_w_qkv_t": jnp.swapaxes(attn_w_qkv, 1, 2).astype(BF16),
        "attn_q_gain": attn_q_gain, "attn_k_gain": attn_k_gain,
        "attn_w_o": attn_w_o.astype(BF16),
        "fnet_w_out": fnet_w_out.astype(BF16), "fnet_b_out": fnet_b_out,
        "conv_w_pw1": conv_w_pw1.astype(BF16), "conv_b_pw1": conv_b_pw1,
        "conv_dw_w": conv_dw_w, "conv_dw_b": conv_dw_b,
        "conv_ln_g": conv_ln_g, "conv_ln_b": conv_ln_b,
        "conv_w_pw2": conv_w_pw2.astype(BF16), "conv_b_pw2": conv_b_pw2,
        "ffn_w_up": ffn_w_up.astype(BF16), "ffn_dw_w": ffn_dw_w, "ffn_dw_b": ffn_dw_b,
        "ffn_w_down": ffn_w_down.astype(BF16),
    }
    return (_run_trunk(x_prompt, p), _run_trunk(x_sample, p))
```

```python
import functools
import math

import numpy as np
import jax
import jax.numpy as jnp
from jax import lax
from jax.experimental import pallas as pl
from jax.experimental.pallas import tpu as pltpu

F32 = jnp.float32
BF16 = jnp.bfloat16

D_MODEL = 2048
DEPTH = 4
N_MIXERS = 3
GRID_W = 64
HEAD_DIM = 128
N_HEADS = D_MODEL // HEAD_DIM
N_KV_HEADS = N_HEADS // 4
GQA_GROUP = N_HEADS // N_KV_HEADS
ROPE_THETA = 10000.0
ROPE_AXIS_DIM = HEAD_DIM // 2
ROPE_F = ROPE_AXIS_DIM // 2
FNET_GROUPS = 4
FNET_GROUP_DIM = D_MODEL // FNET_GROUPS
CONF_KERNEL = 31
CONF_PAD = (CONF_KERNEL - 1) // 2
FFN_DIM = 4 * D_MODEL
EPS = 1e-6

V7X_VMEM_BYTES = 64 * 1024 * 1024
V7X_VMEM_LIMIT = V7X_VMEM_BYTES - 4 * 1024 * 1024
LANES = 128
SUBLANES_F32 = 8
SUBLANES_BF16 = 16
HALO = SUBLANES_BF16

ROW_TILE = 512
FFN_CHUNK = 1024
FFN_SUBCHUNK = 256
CONF_ROW_TILE = 256
CONF_LANE_CHUNK = 512
CONF_ROW_BLOCK = 32
ATTN_TQ = 512
ATTN_TK = 512
ATTN_SLOTS = 4
ATTN_MAX_FIXED_SHIFT = 60.0
ATTN_BOUND_SLACK = 1.02
DFT_INTERLEAVE = SUBLANES_F32


def _params(semantics):
    return pltpu.CompilerParams(dimension_semantics=semantics,
                                vmem_limit_bytes=V7X_VMEM_LIMIT)


def _resident(shape, index_map):
    return pl.BlockSpec(shape, index_map, pipeline_mode=pl.Buffered(1))


def _rms(x, g):
    return x * lax.rsqrt(jnp.mean(x * x, axis=-1, keepdims=True) + EPS) * g


def _sigmoid(x):
    return 1.0 / (1.0 + jnp.exp(-x))


def _gelu_tanh(x):
    a = -2.0 * math.sqrt(2.0 / math.pi) * math.log2(math.e)
    return x / (1.0 + jnp.exp2(x * (a + (a * 0.044715) * (x * x))))


def _dot(a, b):
    return jnp.dot(a, b, preferred_element_type=F32)


def _store_sandwich(m, x_ref, gp_ref, gn_ref, o_ref, h_ref):
    xn = x_ref[...] + _rms(m, gp_ref[...])
    o_ref[...] = xn
    if h_ref is not None:
        h_ref[...] = _rms(xn, gn_ref[...]).astype(BF16)


def _rmsnorm_kernel(x_ref, g_ref, o_ref):
    o_ref[...] = _rms(x_ref[...], g_ref[...]).astype(o_ref.dtype)


def rmsnorm_bf16(x, g):
    t, d = x.shape
    return pl.pallas_call(
        _rmsnorm_kernel,
        grid=(t // ROW_TILE,),
        in_specs=[pl.BlockSpec((ROW_TILE, d), lambda i: (i, 0)),
                  pl.BlockSpec((1, d), lambda i: (0, 0))],
        out_specs=pl.BlockSpec((ROW_TILE, d), lambda i: (i, 0)),
        out_shape=jax.ShapeDtypeStruct((t, d), BF16),
        compiler_params=_params(("parallel",)),
        name="rmsnorm",
    )(x, g.reshape(1, d))


def _proj_res_kernel(*refs, has_bias, has_next):
    a_ref, w_ref = refs[0], refs[1]
    pos = 2
    b_ref = None
    if has_bias:
        b_ref = refs[pos]
        pos += 1
    x_ref, gp_ref = refs[pos], refs[pos + 1]
    pos += 2
    gn_ref = None
    if has_next:
        gn_ref = refs[pos]
        pos += 1
    o_ref = refs[pos]
    h_ref = refs[pos + 1] if has_next else None
    m = _dot(a_ref[...].astype(BF16), w_ref[...])
    if has_bias:
        m = m + b_ref[...]
    _store_sandwich(m, x_ref, gp_ref, gn_ref, o_ref, h_ref)


def proj_residual(a, w, layer, bias, x, g_post, g_next):
    t, k = a.shape
    d = w.shape[2]
    has_bias = bias is not None
    has_next = g_next is not None
    row = lambda i: (i, 0)
    fixed = lambda i: (0, 0)
    in_specs = [pl.BlockSpec((ROW_TILE, k), row),
                _resident((None, k, d), lambda i: (layer, 0, 0))]
    args = [a, w]
    if has_bias:
        in_specs.append(pl.BlockSpec((1, d), fixed))
        args.append(bias.reshape(1, d))
    in_specs += [pl.BlockSpec((ROW_TILE, d), row), pl.BlockSpec((1, d), fixed)]
    args += [x, g_post.reshape(1, d)]
    out_specs = [pl.BlockSpec((ROW_TILE, d), row)]
    out_shape = [jax.ShapeDtypeStruct((t, d), F32)]
    if has_next:
        in_specs.append(pl.BlockSpec((1, d), fixed))
        args.append(g_next.reshape(1, d))
        out_specs.append(pl.BlockSpec((ROW_TILE, d), row))
        out_shape.append(jax.ShapeDtypeStruct((t, d), BF16))
    outs = pl.pallas_call(
        functools.partial(_proj_res_kernel, has_bias=has_bias, has_next=has_next),
        grid=(t // ROW_TILE,),
        in_specs=in_specs, out_specs=out_specs, out_shape=out_shape,
        compiler_params=_params(("parallel",)),
        name="proj_residual",
    )(*args)
    return (outs[0], outs[1]) if has_next else (outs[0], None)


def _halo_specs(tm, d, n_tiles):
    per = tm // HALO
    last = n_tiles * per - 1
    return [pl.BlockSpec((HALO, d), lambda i, *_: (jnp.maximum(i * per - 1, 0), 0)),
            pl.BlockSpec((tm, d), lambda i, *_: (i, 0)),
            pl.BlockSpec((HALO, d), lambda i, *_: (jnp.minimum((i + 1) * per, last), 0))]


def _seq_edges(i, tiles_per_seq):
    pos = i % tiles_per_seq
    return pos == 0, pos == tiles_per_seq - 1


def _ffn_kernel(*refs, tm, tc, tiles_per_seq, n_chunks, has_next):
    (hp_ref, hm_ref, hn_ref, wg_ref, wv_ref, cg_ref, cv_ref, bg_ref, bv_ref,
     wd_ref, x_ref, gp_ref) = refs[:12]
    pos = 12
    gn_ref = None
    if has_next:
        gn_ref = refs[pos]
        pos += 1
    o_ref = refs[pos]
    pos += 1
    h_ref = None
    if has_next:
        h_ref = refs[pos]
        pos += 1
    hh_ref, acc_ref = refs[pos], refs[pos + 1]

    i = pl.program_id(0)
    c = pl.program_id(1)

    @pl.when(c == 0)
    def _():
        first, last = _seq_edges(i, tiles_per_seq)
        keep_prev = jnp.where(first, 0.0, 1.0)
        keep_next = jnp.where(last, 0.0, 1.0)
        hh_ref[0:tm, :] = hm_ref[...]
        r = lax.broadcasted_iota(jnp.int32, (HALO, 1), 0)
        prev_row = hp_ref[...].astype(F32)[HALO - 1:HALO, :] * keep_prev
        next_row = hn_ref[...].astype(F32)[0:1, :] * keep_next
        hh_ref[tm:, :] = jnp.where(r == 0, prev_row, jnp.where(r == 1, next_row, 0.0)
                                   ).astype(BF16)
        acc_ref[...] = jnp.zeros_like(acc_ref)

    half = tm // 2
    row_id = lax.broadcasted_iota(jnp.int32, (tm, 1), 0)

    def up_conv(w_ref, c_ref, b_ref, sl):
        w_up = w_ref[:, sl]
        v_lo = _dot(hh_ref[0:half, :], w_up)
        v_hi = _dot(hh_ref[half:, :], w_up)
        v = jnp.concatenate([v_lo, v_hi[0:half]], axis=0)
        before = jnp.where(row_id == 0, v_hi[half:half + 1], pltpu.roll(v, 1, 0))
        after = jnp.where(row_id == tm - 1, v_hi[half + 1:half + 2], pltpu.roll(v, tm - 1, 0))
        w = c_ref[:, sl]
        return w[0:1, :] * before + w[1:2, :] * v + w[2:3, :] * after + b_ref[:, sl]

    subs = [slice(s * FFN_SUBCHUNK, (s + 1) * FFN_SUBCHUNK) for s in range(tc // FFN_SUBCHUNK)]
    acts = []
    for sl in subs:
        ug = up_conv(wg_ref, cg_ref, bg_ref, sl)
        uv = up_conv(wv_ref, cv_ref, bv_ref, sl)
        acts.append((_gelu_tanh(ug) * uv).astype(BF16))
    for sl, act in zip(subs, acts):
        acc_ref[...] += _dot(act, wd_ref[sl, :])

    @pl.when(c == n_chunks - 1)
    def _():
        _store_sandwich(acc_ref[...], x_ref, gp_ref, gn_ref, o_ref, h_ref)


def conv_ffn_residual(h, w_up, dw_w, dw_b, w_down, layer, x, g_post, g_next, seq_len):
    t, d = h.shape
    f = w_down.shape[1]
    tm, tc = ROW_TILE, FFN_CHUNK
    n_tiles, n_chunks = t // tm, f // tc
    has_next = g_next is not None
    row = lambda i, c: (i, 0)
    fixed = lambda i, c: (0, 0)
    in_specs = _halo_specs(tm, d, n_tiles) + [
        pl.BlockSpec((None, d, tc), lambda i, c: (layer, 0, c)),
        pl.BlockSpec((None, d, tc), lambda i, c: (layer, 0, c + n_chunks)),
        pl.BlockSpec((None, 3, tc), lambda i, c: (layer, 0, c)),
        pl.BlockSpec((None, 3, tc), lambda i, c: (layer, 0, c + n_chunks)),
        pl.BlockSpec((None, 1, tc), lambda i, c: (layer, 0, c)),
        pl.BlockSpec((None, 1, tc), lambda i, c: (layer, 0, c + n_chunks)),
        pl.BlockSpec((None, tc, d), lambda i, c: (layer, c, 0)),
        pl.BlockSpec((tm, d), row),
        pl.BlockSpec((1, d), fixed),
    ]
    dw_b3 = dw_b.reshape(dw_b.shape[0], 1, 2 * f)
    args = [h, h, h, w_up, w_up, dw_w, dw_w, dw_b3, dw_b3, w_down, x, g_post.reshape(1, d)]
    out_specs = [pl.BlockSpec((tm, d), row)]
    out_shape = [jax.ShapeDtypeStruct((t, d), F32)]
    if has_next:
        in_specs.append(pl.BlockSpec((1, d), fixed))
        args.append(g_next.reshape(1, d))
        out_specs.append(pl.BlockSpec((tm, d), row))
        out_shape.append(jax.ShapeDtypeStruct((t, d), BF16))
    outs = pl.pallas_call(
        functools.partial(_ffn_kernel, tm=tm, tc=tc, tiles_per_seq=seq_len // tm,
                          n_chunks=n_chunks, has_next=has_next),
        grid=(n_tiles, n_chunks),
        in_specs=in_specs, out_specs=out_specs, out_shape=out_shape,
        scratch_shapes=[pltpu.VMEM((tm + HALO, d), BF16), pltpu.VMEM((tm, d), F32)],
        compiler_params=_params(("parallel", "arbitrary")),
        name="conv_ffn",
    )(*args)
    return (outs[0], outs[1]) if has_next else (outs[0], None)


def _head_norm_rope(x, gain, cr, sr, cc, sc):
    y = x * lax.rsqrt(jnp.mean(x * x, axis=0, keepdims=True) + EPS) * gain
    f = ROPE_F
    r1, r2, c1, c2 = y[0:f], y[f:2 * f], y[2 * f:3 * f], y[3 * f:4 * f]
    return jnp.concatenate([r1 * cr - r2 * sr, r2 * cr + r1 * sr,
                            c1 * cc - c2 * sc, c2 * cc + c1 * sc], axis=0)


def _qkv_kernel(h_ref, wt_ref, gq_ref, gk_ref, cr_ref, sr_ref, cc_ref, sc_ref,
                q_ref, k_ref, v_ref, *, tn, q_scale):
    h = h_ref[...]
    rope = (cr_ref[...], sr_ref[...], cc_ref[...], sc_ref[...])
    grp = GQA_GROUP * HEAD_DIM
    nt_dims = (((1,), (1,)), ((), ()))

    def proj(row0):
        return lax.dot_general(wt_ref[row0:row0 + grp, :], h, nt_dims,
                               preferred_element_type=F32)

    gq = gq_ref[...]
    for kv in range(N_KV_HEADS):
        r = proj(kv * grp)
        for g in range(GQA_GROUP):
            y = _head_norm_rope(r[g * HEAD_DIM:(g + 1) * HEAD_DIM], gq, *rope) * q_scale
            y = y.astype(BF16)
            for j in range(tn // ATTN_TQ):
                col = (j * GQA_GROUP + g) * ATTN_TQ
                q_ref[kv * HEAD_DIM:(kv + 1) * HEAD_DIM, col:col + ATTN_TQ] = (
                    y[:, j * ATTN_TQ:(j + 1) * ATTN_TQ])
    gk = gk_ref[...]
    r = proj(N_HEADS * HEAD_DIM)
    for kv in range(N_KV_HEADS):
        y = _head_norm_rope(r[kv * HEAD_DIM:(kv + 1) * HEAD_DIM], gk, *rope)
        k_ref[:, kv * HEAD_DIM:(kv + 1) * HEAD_DIM] = y.T.astype(BF16)
    r = proj((N_HEADS + N_KV_HEADS) * HEAD_DIM)
    ones = jnp.ones((SUBLANES_BF16, tn), BF16)
    for kv in range(N_KV_HEADS):
        v_ref[kv, 0, 0:HEAD_DIM, :] = r[kv * HEAD_DIM:(kv + 1) * HEAD_DIM].astype(BF16)
        v_ref[kv, 0, HEAD_DIM:, :] = ones


def qkv_project(h, w_t, layer, q_gain, k_gain, rope_t, seq_len):
    t, d = h.shape
    tn = ATTN_TK
    per_seq = seq_len // tn
    q_scale = HEAD_DIM ** -0.5 * math.log2(math.e)
    gq = jnp.broadcast_to(q_gain.reshape(HEAD_DIM, 1), (HEAD_DIM, tn))
    gk = jnp.broadcast_to(k_gain.reshape(HEAD_DIM, 1), (HEAD_DIM, tn))
    rope_spec = pl.BlockSpec((ROPE_F, tn), lambda i: (0, i % per_seq))
    gain_spec = pl.BlockSpec((HEAD_DIM, tn), lambda i: (0, 0))
    return pl.pallas_call(
        functools.partial(_qkv_kernel, tn=tn, q_scale=q_scale),
        grid=(t // tn,),
        in_specs=[pl.BlockSpec((tn, d), lambda i: (i, 0)),
                  _resident((None,) + w_t.shape[1:], lambda i: (layer, 0, 0)),
                  gain_spec, gain_spec, rope_spec, rope_spec, rope_spec, rope_spec],
        out_specs=[pl.BlockSpec((N_KV_HEADS * HEAD_DIM, GQA_GROUP * tn), lambda i: (0, i)),
                   pl.BlockSpec((tn, N_KV_HEADS * HEAD_DIM), lambda i: (i, 0)),
                   pl.BlockSpec((N_KV_HEADS, 1, HEAD_DIM + SUBLANES_BF16, tn),
                                lambda i: (0, i, 0, 0))],
        out_shape=[jax.ShapeDtypeStruct((N_KV_HEADS * HEAD_DIM, GQA_GROUP * t), BF16),
                   jax.ShapeDtypeStruct((t, N_KV_HEADS * HEAD_DIM), BF16),
                   jax.ShapeDtypeStruct((N_KV_HEADS, t // tn, HEAD_DIM + SUBLANES_BF16, tn),
                                        BF16)],
        compiler_params=_params(("parallel",)),
        name="qkv_project",
    )(h, w_t, gq, gk, *rope_t)


def _attn_online(q, k_ref, v_ref, m_ref, acc_ref, s_ref, mb_ref, p_ref, al_ref, n_kblocks):
    last_slot = ATTN_SLOTS - 1

    def scores(start, slot):
        s = _dot(k_ref[pl.ds(start, ATTN_TK), :], q)
        s_ref[slot] = s
        mb_ref[slot] = jnp.max(s, axis=0, keepdims=True)

    m_ref[...] = jnp.full_like(m_ref, -jnp.inf)
    acc_ref[...] = jnp.zeros_like(acc_ref)
    p_ref[last_slot] = jnp.zeros(p_ref.shape[1:], p_ref.dtype)
    al_ref[last_slot] = jnp.ones(al_ref.shape[1:], al_ref.dtype)
    scores(0, 0)

    def step(c, slot):
        nxt, prv = (slot + 1) % ATTN_SLOTS, (slot - 1) % ATTN_SLOTS
        c_next = jnp.minimum(c + 1, n_kblocks - 1)
        scores(pl.multiple_of(c_next * ATTN_TK, ATTN_TK), nxt)
        c_prev = jnp.maximum(c - 1, 0)
        acc_ref[...] = acc_ref[...] * al_ref[prv] + _dot(v_ref[0, c_prev], p_ref[prv])
        m_old = m_ref[...]
        m_new = jnp.maximum(m_old, mb_ref[slot])
        p_ref[slot] = jnp.exp2(s_ref[slot] - m_new).astype(p_ref.dtype)
        al_ref[slot] = jnp.exp2(m_old - m_new)
        m_ref[...] = m_new

    def body(i, carry):
        for j in range(ATTN_SLOTS):
            step(ATTN_SLOTS * i + j, j)
        return carry

    lax.fori_loop(0, n_kblocks // ATTN_SLOTS, body, 0)
    acc_ref[...] = (acc_ref[...] * al_ref[last_slot]
                    + _dot(v_ref[0, n_kblocks - 1], p_ref[last_slot]))


def _attn_fixed_shift(q, k_ref, v_ref, acc_ref, shift, n_kblocks):
    acc_ref[...] = jnp.zeros_like(acc_ref)

    def body(i, carry):
        acc = acc_ref[...]
        for j in range(ATTN_SLOTS):
            c = ATTN_SLOTS * i + j
            start = pl.multiple_of(c * ATTN_TK, ATTN_TK)
            s = _dot(k_ref[pl.ds(start, ATTN_TK), :], q)
            acc = acc + _dot(v_ref[0, c], jnp.exp2(s - shift).astype(BF16))
        acc_ref[...] = acc
        return carry

    lax.fori_loop(0, n_kblocks // ATTN_SLOTS, body, 0)


def _attn_kernel(bound_ref, q_ref, k_ref, v_ref, o_ref, m_ref, acc_ref, s_ref, mb_ref, p_ref,
                 al_ref, *, n_kblocks):
    q = q_ref[...]
    bound = bound_ref[0, 0]

    small = bound <= ATTN_MAX_FIXED_SHIFT

    @pl.when(small)
    def _():
        _attn_fixed_shift(q, k_ref, v_ref, acc_ref, bound, n_kblocks)

    @pl.when(jnp.logical_not(small))
    def _():
        _attn_online(q, k_ref, v_ref, m_ref, acc_ref, s_ref, mb_ref, p_ref, al_ref, n_kblocks)

    acc = acc_ref[...]
    out = acc[0:HEAD_DIM] / acc[HEAD_DIM:HEAD_DIM + 1]
    for g in range(GQA_GROUP):
        o_ref[:, g * HEAD_DIM:(g + 1) * HEAD_DIM] = (
            out[:, g * ATTN_TQ:(g + 1) * ATTN_TQ].T.astype(o_ref.dtype))


def score_bound(q_gain, k_gain):
    b = (HEAD_DIM ** 0.5 * math.log2(math.e)) * jnp.max(jnp.abs(q_gain)) * jnp.max(jnp.abs(k_gain))
    return (b * ATTN_BOUND_SLACK).reshape(1, 1).astype(F32)


def flash_attention(q, k, v, bound, batch, seq_len):
    t = batch * seq_len
    q_per_seq = seq_len // ATTN_TQ
    n_kblocks = seq_len // ATTN_TK
    assert n_kblocks % ATTN_SLOTS == 0
    lanes = GQA_GROUP * ATTN_TQ
    return pl.pallas_call(
        functools.partial(_attn_kernel, n_kblocks=n_kblocks),
        grid=(batch, N_KV_HEADS, q_per_seq),
        in_specs=[pl.BlockSpec(memory_space=pltpu.SMEM),
                  pl.BlockSpec((HEAD_DIM, lanes), lambda b, kv, qi: (kv, b * q_per_seq + qi)),
                  pl.BlockSpec((seq_len, HEAD_DIM), lambda b, kv, qi: (b, kv)),
                  pl.BlockSpec((1, n_kblocks, HEAD_DIM + SUBLANES_BF16, ATTN_TK),
                               lambda b, kv, qi: (kv, b, 0, 0))],
        out_specs=pl.BlockSpec((ATTN_TQ, GQA_GROUP * HEAD_DIM),
                               lambda b, kv, qi: (b * q_per_seq + qi, kv)),
        out_shape=jax.ShapeDtypeStruct((t, N_HEADS * HEAD_DIM), BF16),
        scratch_shapes=[pltpu.VMEM((1, lanes), F32),
                        pltpu.VMEM((HEAD_DIM + SUBLANES_BF16, lanes), F32),
                        pltpu.VMEM((ATTN_SLOTS, ATTN_TK, lanes), F32),
                        pltpu.VMEM((ATTN_SLOTS, 1, lanes), F32),
                        pltpu.VMEM((ATTN_SLOTS, ATTN_TK, lanes), BF16),
                        pltpu.VMEM((ATTN_SLOTS, 1, lanes), F32)],
        compiler_params=_params(("parallel", "parallel", "parallel")),
        name="flash_attention",
    )(bound, q, k, v)


def _rope_tables_t(n):
    rows = n // GRID_W
    row = jnp.repeat(jnp.arange(rows, dtype=F32), GRID_W)
    col = jnp.tile(jnp.arange(GRID_W, dtype=F32), rows)
    inv_freq = ROPE_THETA ** (-jnp.arange(0, ROPE_AXIS_DIM, 2, dtype=F32) / ROPE_AXIS_DIM)
    ang_r = inv_freq[:, None] * row[None, :]
    ang_c = inv_freq[:, None] * col[None, :]
    return jnp.cos(ang_r), jnp.sin(ang_r), jnp.cos(ang_c), jnp.sin(ang_c)


def _dft_split(n):
    n2 = max(1, int(round(math.sqrt(n) / 4)))
    while n % n2:
        n2 -= 1
    return n // n2, n2


def _dft_tables(n):
    n1, n2 = _dft_split(n)
    k1 = np.arange(n1)
    ang1 = 2.0 * np.pi * ((k1[:, None] * k1[None, :]) % n1) / n1
    m1 = np.concatenate([np.cos(ang1), -np.sin(ang1)], axis=0)
    t2 = np.arange(n2)
    ang_tw = 2.0 * np.pi * (t2[:, None] * k1[None, :]) / n
    ang2 = 2.0 * np.pi * ((t2[:, None] * t2[None, :]) % n2) / n2
    eye = np.eye(DFT_INTERLEAVE)
    kc = np.einsum("kt,ij->kijt", np.cos(ang2), eye).reshape(n2 * DFT_INTERLEAVE, -1)
    ks = np.einsum("kt,ij->kijt", np.sin(ang2), eye).reshape(n2 * DFT_INTERLEAVE, -1)
    k_re = np.concatenate([kc, ks], axis=1)
    k_im = np.concatenate([-ks, kc], axis=1)
    c = np.arange(FNET_GROUP_DIM)
    ang_c = 2.0 * np.pi * ((c[:, None] * c[None, :]) % FNET_GROUP_DIM) / FNET_GROUP_DIM
    as_bf16 = lambda a: jnp.asarray(a, F32).astype(BF16)
    tw_shape = (n2, n1, LANES)
    return dict(
        n1=n1, n2=n2, m1=as_bf16(m1), k_re=as_bf16(k_re), k_im=as_bf16(k_im),
        cc=as_bf16(np.cos(ang_c)), sc=as_bf16(np.sin(ang_c)),
        tw_cos=jnp.broadcast_to(jnp.asarray(np.cos(ang_tw), F32)[:, :, None], tw_shape),
        tw_sin=jnp.broadcast_to(jnp.asarray(np.sin(ang_tw), F32)[:, :, None], tw_shape))


def _dft1_kernel(x_ref, m1_ref, twc_ref, tws_ref, o_ref, *, n1):
    r = _dot(m1_ref[...], x_ref[...])
    twc, tws = twc_ref[0], tws_ref[0]
    for j in range(x_ref.shape[1] // LANES):
        sl = slice(j * LANES, (j + 1) * LANES)
        ar, ai = r[0:n1, sl], r[n1:, sl]
        o_ref[0, 0, :, sl] = (ar * twc + ai * tws).astype(o_ref.dtype)
        o_ref[0, 1, :, sl] = (ai * twc - ar * tws).astype(o_ref.dtype)


def _dft2_kernel(a_ref, kre_ref, kim_ref, cc_ref, sc_ref, o_ref, *, rows, n2, scale):
    a = a_ref[0].reshape(2 * rows, a_ref.shape[-1])
    zr = _dot(kre_ref[...], a).astype(BF16)
    zi = _dot(kim_ref[...], a).astype(BF16)
    cc, sc = cc_ref[...], sc_ref[...]
    for g in range(FNET_GROUPS):
        sl = slice(g * FNET_GROUP_DIM, (g + 1) * FNET_GROUP_DIM)
        y = (_dot(zr[:, sl], cc) + _dot(zi[:, sl], sc)) * scale
        o_ref[0, :, :, sl] = y.reshape(n2, DFT_INTERLEAVE, FNET_GROUP_DIM)


def fourier_real_2d(h, batch, seq_len, tabs):
    t, d = h.shape
    n1, n2 = tabs["n1"], tabs["n2"]
    rows = n2 * DFT_INTERLEAVE
    a2 = pl.pallas_call(
        functools.partial(_dft1_kernel, n1=n1),
        grid=(batch, n2),
        in_specs=[pl.BlockSpec((n1, d), lambda b, j: (b, j)),
                  _resident((2 * n1, n1), lambda b, j: (0, 0)),
                  pl.BlockSpec((1, n1, LANES), lambda b, j: (j, 0, 0)),
                  pl.BlockSpec((1, n1, LANES), lambda b, j: (j, 0, 0))],
        out_specs=pl.BlockSpec((1, 2, n1, d), lambda b, j: (b, 0, 0, j)),
        out_shape=jax.ShapeDtypeStruct((batch, 2, n1, n2 * d), BF16),
        compiler_params=_params(("parallel", "parallel")),
        name="dft_stage1",
    )(h.reshape(batch * n1, n2 * d), tabs["m1"], tabs["tw_cos"], tabs["tw_sin"])
    scale = 1.0 / math.sqrt(seq_len * FNET_GROUP_DIM)
    y = pl.pallas_call(
        functools.partial(_dft2_kernel, rows=rows, n2=n2, scale=scale),
        grid=(batch, n1 // DFT_INTERLEAVE),
        in_specs=[pl.BlockSpec((1, 2, rows, d), lambda b, j: (b, 0, j, 0)),
                  _resident((rows, 2 * rows), lambda b, j: (0, 0)),
                  _resident((rows, 2 * rows), lambda b, j: (0, 0)),
                  _resident((FNET_GROUP_DIM, FNET_GROUP_DIM), lambda b, j: (0, 0)),
                  _resident((FNET_GROUP_DIM, FNET_GROUP_DIM), lambda b, j: (0, 0))],
        out_specs=pl.BlockSpec((1, n2, DFT_INTERLEAVE, d), lambda b, j: (b, 0, j, 0)),
        out_shape=jax.ShapeDtypeStruct((batch, n2, n1, d), F32),
        compiler_params=_params(("parallel", "parallel")),
        name="dft_stage2",
    )(a2.reshape(batch, 2, n1 * n2, d), tabs["k_re"], tabs["k_im"], tabs["cc"], tabs["sc"])
    return y.reshape(t, d)


def _conf_kernel(hp_ref, hm_ref, hn_ref, w1_ref, b1_ref, dw_ref, db_ref, lg_ref, lb_ref,
                 o_ref, hh_ref, y_ref, sh_ref, *, tm, tiles_per_seq):
    d = o_ref.shape[1]
    i = pl.program_id(0)
    first, last = _seq_edges(i, tiles_per_seq)
    hh_ref[0:HALO, :] = hp_ref[...]
    hh_ref[HALO:HALO + tm, :] = hm_ref[...]
    hh_ref[HALO + tm:, :] = hn_ref[...]
    a = hh_ref[...]
    ext = tm + 2 * HALO
    rows = lax.broadcasted_iota(jnp.int32, (ext, 1), 0)
    lo = jnp.where(first, HALO, 0)
    hi = jnp.where(last, HALO + tm, ext)
    inside = (rows >= lo) & (rows < hi)
    span = tm + SUBLANES_F32 * (2 * HALO // SUBLANES_F32 - 1)
    for j in range(d // CONF_LANE_CHUNK):
        sl = slice(j * CONF_LANE_CHUNK, (j + 1) * CONF_LANE_CHUNK)
        sg = slice(d + j * CONF_LANE_CHUNK, d + (j + 1) * CONF_LANE_CHUNK)
        za = _dot(a, w1_ref[:, sl]) + b1_ref[:, sl]
        zg = _dot(a, w1_ref[:, sg]) + b1_ref[:, sg]
        u = jnp.where(inside, za * _sigmoid(zg), 0.0)
        for b in range(SUBLANES_F32):
            sh_ref[b] = u[b:b + span]
        sub = CONF_ROW_BLOCK // SUBLANES_F32
        for cc in range(CONF_LANE_CHUNK // LANES):
            lanes = slice(cc * LANES, (cc + 1) * LANES)
            col = slice(j * CONF_LANE_CHUNK + cc * LANES, j * CONF_LANE_CHUNK + (cc + 1) * LANES)
            taps = [jnp.broadcast_to(dw_ref[k:k + 1, col], (SUBLANES_F32, LANES))
                    for k in range(CONF_KERNEL)]
            bias = jnp.broadcast_to(db_ref[:, col], (SUBLANES_F32, LANES))
            for r0 in range(0, tm, CONF_ROW_BLOCK):
                acc = jnp.broadcast_to(bias[None], (sub, SUBLANES_F32, LANES))
                for k in range(CONF_KERNEL):
                    off = HALO - CONF_PAD + k
                    base = (off // SUBLANES_F32) * SUBLANES_F32 + r0
                    win = sh_ref[off % SUBLANES_F32, base:base + CONF_ROW_BLOCK, lanes]
                    acc = acc + taps[k][None] * win.reshape(sub, SUBLANES_F32, LANES)
                y_ref[r0:r0 + CONF_ROW_BLOCK, col] = acc.reshape(CONF_ROW_BLOCK, LANES)
    y = y_ref[...]
    mu = jnp.mean(y, axis=-1, keepdims=True)
    yc = y - mu
    yn = yc * lax.rsqrt(jnp.mean(yc * yc, axis=-1, keepdims=True) + EPS) * lg_ref[...] + lb_ref[...]
    o_ref[...] = (yn * _sigmoid(yn)).astype(o_ref.dtype)


def conformer_inner(h, w1, layer, b1, dw_w, dw_b, ln_g, ln_b, seq_len):
    t, d = h.shape
    tm = CONF_ROW_TILE
    n_tiles = t // tm
    fixed = lambda i: (0, 0)
    return pl.pallas_call(
        functools.partial(_conf_kernel, tm=tm, tiles_per_seq=seq_len // tm),
        grid=(n_tiles,),
        in_specs=_halo_specs(tm, d, n_tiles) + [
            _resident((None, d, 2 * d), lambda i: (layer, 0, 0)), pl.BlockSpec((1, 2 * d), fixed),
            pl.BlockSpec((CONF_KERNEL, d), fixed), pl.BlockSpec((1, d), fixed),
            pl.BlockSpec((1, d), fixed), pl.BlockSpec((1, d), fixed)],
        out_specs=pl.BlockSpec((tm, d), lambda i: (i, 0)),
        out_shape=jax.ShapeDtypeStruct((t, d), BF16),
        scratch_shapes=[pltpu.VMEM((tm + 2 * HALO, d), BF16), pltpu.VMEM((tm, d), F32),
                        pltpu.VMEM((SUBLANES_F32, tm + 2 * HALO - SUBLANES_F32, CONF_LANE_CHUNK),
                                   F32)],
        compiler_params=_params(("parallel",)),
        name="conformer_inner",
    )(h, h, h, w1, b1.reshape(1, 2 * d), dw_w, dw_b.reshape(1, d),
      ln_g.reshape(1, d), ln_b.reshape(1, d))


def _run_trunk(x, p):
    batch, seq_len, d = x.shape
    t = batch * seq_len
    xf = x.reshape(t, d)
    rope_t = _rope_tables_t(seq_len)
    tabs = _dft_tables(seq_len)
    h = rmsnorm_bf16(xf, p["norm_mix_pre"][0])
    for i in range(DEPTH):
        j, kind = i // N_MIXERS, i % N_MIXERS
        g_post, g_ffn = p["norm_mix_post"][i], p["norm_ffn_pre"][i]
        if kind == 0:
            q, k, v = qkv_project(h, p["attn_w_qkv_t"], j, p["attn_q_gain"][j],
                                  p["attn_k_gain"][j], rope_t, seq_len)
            bound = score_bound(p["attn_q_gain"][j], p["attn_k_gain"][j])
            o = flash_attention(q, k, v, bound, batch, seq_len)
            xf, h = proj_residual(o, p["attn_w_o"], j, None, xf, g_post, g_ffn)
        elif kind == 1:
            y = fourier_real_2d(h, batch, seq_len, tabs)
            xf, h = proj_residual(y, p["fnet_w_out"], j, p["fnet_b_out"][j], xf, g_post, g_ffn)
        else:
            u = conformer_inner(h, p["conv_w_pw1"], j, p["conv_b_pw1"][j], p["conv_dw_w"][j],
                                p["conv_dw_b"][j], p["conv_ln_g"][j], p["conv_ln_b"][j], seq_len)
            xf, h = proj_residual(u, p["conv_w_pw2"], j, p["conv_b_pw2"][j], xf, g_post, g_ffn)
        g_next = p["norm_mix_pre"][i + 1] if i + 1 < DEPTH else None
        xf, h = conv_ffn_residual(h, p["ffn_w_up"], p["ffn_dw_w"], p["ffn_dw_b"],
                                  p["ffn_w_down"], i, xf, p["norm_ffn_post"][i], g_next, seq_len)
    return xf.reshape(batch, seq_len, d)


def kernel(x_prompt, x_sample, norm_mix_pre, norm_mix_post, norm_ffn_pre, norm_ffn_post, attn_w_qkv, attn_q_gain, attn_k_gain, attn_w_o, fnet_w_out, fnet_b_out, conv_w_pw1, conv_b_pw1, conv_dw_w, conv_dw_b, conv_ln_g, conv_ln_b, conv_w_pw2, conv_b_pw2, ffn_w_up, ffn_dw_w, ffn_dw_b, ffn_w_down):
    p = {
        "norm_mix_pre": norm_mix_pre, "norm_mix_post": norm_mix_post,
        "norm_ffn_pre": norm_ffn_pre, "norm_ffn_post": norm_ffn_post,
        "attn_w_qkv_t": jnp.swapaxes(attn_w_qkv, 1, 2).astype(BF16),
        "attn_q_gain": attn_q_gain, "attn_k_gain": attn_k_gain,
        "attn_w_o": attn_w_o.astype(BF16),
        "fnet_w_out": fnet_w_out.astype(BF16), "fnet_b_out": fnet_b_out,
        "conv_w_pw1": conv_w_pw1.astype(BF16), "conv_b_pw1": conv_b_pw1,
        "conv_dw_w": conv_dw_w, "conv_dw_b": conv_dw_b,
        "conv_ln_g": conv_ln_g, "conv_ln_b": conv_ln_b,
        "conv_w_pw2": conv_w_pw2.astype(BF16), "conv_b_pw2": conv_b_pw2,
        "ffn_w_up": ffn_w_up.astype(BF16), "ffn_dw_w": ffn_dw_w, "ffn_dw_b": ffn_dw_b,
        "ffn_w_down": ffn_w_down.astype(BF16),
    }
    return (_run_trunk(x_prompt, p), _run_trunk(x_sample, p))
```

```python
import functools
import math

import numpy as np
import jax
import jax.numpy as jnp
from jax import lax
from jax.experimental import pallas as pl
from jax.experimental.pallas import tpu as pltpu

F32 = jnp.float32
BF16 = jnp.bfloat16

D_MODEL = 2048
DEPTH = 4
N_MIXERS = 3
GRID_W = 64
HEAD_DIM = 128
N_HEADS = D_MODEL // HEAD_DIM
N_KV_HEADS = N_HEADS // 4
GQA_GROUP = N_HEADS // N_KV_HEADS
ROPE_THETA = 10000.0
ROPE_AXIS_DIM = HEAD_DIM // 2
ROPE_F = ROPE_AXIS_DIM // 2
FNET_GROUPS = 4
FNET_GROUP_DIM = D_MODEL // FNET_GROUPS
CONF_KERNEL = 31
CONF_PAD = (CONF_KERNEL - 1) // 2
FFN_DIM = 4 * D_MODEL
EPS = 1e-6

V7X_VMEM_BYTES = 64 * 1024 * 1024
V7X_VMEM_LIMIT = V7X_VMEM_BYTES - 4 * 1024 * 1024
LANES = 128
SUBLANES_F32 = 8
SUBLANES_BF16 = 16
HALO = SUBLANES_BF16

ROW_TILE = 512
FFN_CHUNK = 1024
FFN_SUBCHUNK = 256
CONF_ROW_TILE = 256
CONF_LANE_CHUNK = 512
CONF_ROW_BLOCK = 32
QKV_TILE = 1024
ATTN_TQ = 512
ATTN_TK = 512
ATTN_SLOTS = 4
ATTN_MAX_FIXED_SHIFT = 60.0
ATTN_BOUND_SLACK = 1.02
DFT_INTERLEAVE = SUBLANES_F32


def _params(semantics):
    return pltpu.CompilerParams(dimension_semantics=semantics,
                                vmem_limit_bytes=V7X_VMEM_LIMIT)


def _resident(shape, index_map):
    return pl.BlockSpec(shape, index_map, pipeline_mode=pl.Buffered(1))


def _rms(x, g):
    return x * lax.rsqrt(jnp.mean(x * x, axis=-1, keepdims=True) + EPS) * g


def _sigmoid(x):
    return 1.0 / (1.0 + jnp.exp(-x))


def _gelu_tanh(x):
    a = -2.0 * math.sqrt(2.0 / math.pi) * math.log2(math.e)
    return x / (1.0 + jnp.exp2(x * (a + (a * 0.044715) * (x * x))))


def _dot(a, b):
    return jnp.dot(a, b, preferred_element_type=F32)


def _store_sandwich(m, x_ref, gp_ref, gn_ref, o_ref, h_ref):
    xn = x_ref[...] + _rms(m, gp_ref[...])
    o_ref[...] = xn
    if h_ref is not None:
        h_ref[...] = _rms(xn, gn_ref[...]).astype(BF16)


def _rmsnorm_kernel(x_ref, g_ref, o_ref):
    o_ref[...] = _rms(x_ref[...], g_ref[...]).astype(o_ref.dtype)


def rmsnorm_bf16(x, g):
    t, d = x.shape
    return pl.pallas_call(
        _rmsnorm_kernel,
        grid=(t // ROW_TILE,),
        in_specs=[pl.BlockSpec((ROW_TILE, d), lambda i: (i, 0)),
                  pl.BlockSpec((1, d), lambda i: (0, 0))],
        out_specs=pl.BlockSpec((ROW_TILE, d), lambda i: (i, 0)),
        out_shape=jax.ShapeDtypeStruct((t, d), BF16),
        compiler_params=_params(("parallel",)),
        name="rmsnorm",
    )(x, g.reshape(1, d))


def _proj_res_kernel(*refs, has_bias, has_next):
    a_ref, w_ref = refs[0], refs[1]
    pos = 2
    b_ref = None
    if has_bias:
        b_ref = refs[pos]
        pos += 1
    x_ref, gp_ref = refs[pos], refs[pos + 1]
    pos += 2
    gn_ref = None
    if has_next:
        gn_ref = refs[pos]
        pos += 1
    o_ref = refs[pos]
    h_ref = refs[pos + 1] if has_next else None
    m = _dot(a_ref[...].astype(BF16), w_ref[...])
    if has_bias:
        m = m + b_ref[...]
    _store_sandwich(m, x_ref, gp_ref, gn_ref, o_ref, h_ref)


def proj_residual(a, w, layer, bias, x, g_post, g_next):
    t, k = a.shape
    d = w.shape[2]
    has_bias = bias is not None
    has_next = g_next is not None
    row = lambda i: (i, 0)
    fixed = lambda i: (0, 0)
    in_specs = [pl.BlockSpec((ROW_TILE, k), row),
                _resident((None, k, d), lambda i: (layer, 0, 0))]
    args = [a, w]
    if has_bias:
        in_specs.append(pl.BlockSpec((1, d), fixed))
        args.append(bias.reshape(1, d))
    in_specs += [pl.BlockSpec((ROW_TILE, d), row), pl.BlockSpec((1, d), fixed)]
    args += [x, g_post.reshape(1, d)]
    out_specs = [pl.BlockSpec((ROW_TILE, d), row)]
    out_shape = [jax.ShapeDtypeStruct((t, d), F32)]
    if has_next:
        in_specs.append(pl.BlockSpec((1, d), fixed))
        args.append(g_next.reshape(1, d))
        out_specs.append(pl.BlockSpec((ROW_TILE, d), row))
        out_shape.append(jax.ShapeDtypeStruct((t, d), BF16))
    outs = pl.pallas_call(
        functools.partial(_proj_res_kernel, has_bias=has_bias, has_next=has_next),
        grid=(t // ROW_TILE,),
        in_specs=in_specs, out_specs=out_specs, out_shape=out_shape,
        compiler_params=_params(("parallel",)),
        name="proj_residual",
    )(*args)
    return (outs[0], outs[1]) if has_next else (outs[0], None)


def _halo_specs(tm, d, n_tiles):
    per = tm // HALO
    last = n_tiles * per - 1
    return [pl.BlockSpec((HALO, d), lambda i, *_: (jnp.maximum(i * per - 1, 0), 0)),
            pl.BlockSpec((tm, d), lambda i, *_: (i, 0)),
            pl.BlockSpec((HALO, d), lambda i, *_: (jnp.minimum((i + 1) * per, last), 0))]


def _seq_edges(i, tiles_per_seq):
    pos = i % tiles_per_seq
    return pos == 0, pos == tiles_per_seq - 1


def _ffn_kernel(*refs, tm, tc, tiles_per_seq, n_chunks, has_next):
    (hp_ref, hm_ref, hn_ref, wg_ref, wv_ref, cg_ref, cv_ref, bg_ref, bv_ref,
     wd_ref, x_ref, gp_ref) = refs[:12]
    pos = 12
    gn_ref = None
    if has_next:
        gn_ref = refs[pos]
        pos += 1
    o_ref = refs[pos]
    pos += 1
    h_ref = None
    if has_next:
        h_ref = refs[pos]
        pos += 1
    hh_ref, acc_ref = refs[pos], refs[pos + 1]

    i = pl.program_id(0)
    c = pl.program_id(1)

    @pl.when(c == 0)
    def _():
        first, last = _seq_edges(i, tiles_per_seq)
        keep_prev = jnp.where(first, 0.0, 1.0)
        keep_next = jnp.where(last, 0.0, 1.0)
        hh_ref[0:tm, :] = hm_ref[...]
        r = lax.broadcasted_iota(jnp.int32, (HALO, 1), 0)
        prev_row = hp_ref[...].astype(F32)[HALO - 1:HALO, :] * keep_prev
        next_row = hn_ref[...].astype(F32)[0:1, :] * keep_next
        hh_ref[tm:, :] = jnp.where(r == 0, prev_row, jnp.where(r == 1, next_row, 0.0)
                                   ).astype(BF16)
        acc_ref[...] = jnp.zeros_like(acc_ref)

    half = tm // 2
    row_id = lax.broadcasted_iota(jnp.int32, (tm, 1), 0)

    def up_conv(w_ref, c_ref, b_ref, sl):
        w_up = w_ref[:, sl]
        v_lo = _dot(hh_ref[0:half, :], w_up)
        v_hi = _dot(hh_ref[half:, :], w_up)
        v = jnp.concatenate([v_lo, v_hi[0:half]], axis=0)
        before = jnp.where(row_id == 0, v_hi[half:half + 1], pltpu.roll(v, 1, 0))
        after = jnp.where(row_id == tm - 1, v_hi[half + 1:half + 2], pltpu.roll(v, tm - 1, 0))
        w = c_ref[:, sl]
        return w[0:1, :] * before + w[1:2, :] * v + w[2:3, :] * after + b_ref[:, sl]

    subs = [slice(s * FFN_SUBCHUNK, (s + 1) * FFN_SUBCHUNK) for s in range(tc // FFN_SUBCHUNK)]
    acts = []
    for sl in subs:
        ug = up_conv(wg_ref, cg_ref, bg_ref, sl)
        uv = up_conv(wv_ref, cv_ref, bv_ref, sl)
        acts.append((_gelu_tanh(ug) * uv).astype(BF16))
    for sl, act in zip(subs, acts):
        acc_ref[...] += _dot(act, wd_ref[sl, :])

    @pl.when(c == n_chunks - 1)
    def _():
        _store_sandwich(acc_ref[...], x_ref, gp_ref, gn_ref, o_ref, h_ref)


def conv_ffn_residual(h, w_up, dw_w, dw_b, w_down, layer, x, g_post, g_next, seq_len):
    t, d = h.shape
    f = w_down.shape[1]
    tm, tc = ROW_TILE, FFN_CHUNK
    n_tiles, n_chunks = t // tm, f // tc
    has_next = g_next is not None
    row = lambda i, c: (i, 0)
    fixed = lambda i, c: (0, 0)
    in_specs = _halo_specs(tm, d, n_tiles) + [
        pl.BlockSpec((None, d, tc), lambda i, c: (layer, 0, c)),
        pl.BlockSpec((None, d, tc), lambda i, c: (layer, 0, c + n_chunks)),
        pl.BlockSpec((None, 3, tc), lambda i, c: (layer, 0, c)),
        pl.BlockSpec((None, 3, tc), lambda i, c: (layer, 0, c + n_chunks)),
        pl.BlockSpec((None, 1, tc), lambda i, c: (layer, 0, c)),
        pl.BlockSpec((None, 1, tc), lambda i, c: (layer, 0, c + n_chunks)),
        pl.BlockSpec((None, tc, d), lambda i, c: (layer, c, 0)),
        pl.BlockSpec((tm, d), row),
        pl.BlockSpec((1, d), fixed),
    ]
    dw_b3 = dw_b.reshape(dw_b.shape[0], 1, 2 * f)
    args = [h, h, h, w_up, w_up, dw_w, dw_w, dw_b3, dw_b3, w_down, x, g_post.reshape(1, d)]
    out_specs = [pl.BlockSpec((tm, d), row)]
    out_shape = [jax.ShapeDtypeStruct((t, d), F32)]
    if has_next:
        in_specs.append(pl.BlockSpec((1, d), fixed))
        args.append(g_next.reshape(1, d))
        out_specs.append(pl.BlockSpec((tm, d), row))
        out_shape.append(jax.ShapeDtypeStruct((t, d), BF16))
    outs = pl.pallas_call(
        functools.partial(_ffn_kernel, tm=tm, tc=tc, tiles_per_seq=seq_len // tm,
                          n_chunks=n_chunks, has_next=has_next),
        grid=(n_tiles, n_chunks),
        in_specs=in_specs, out_specs=out_specs, out_shape=out_shape,
        scratch_shapes=[pltpu.VMEM((tm + HALO, d), BF16), pltpu.VMEM((tm, d), F32)],
        compiler_params=_params(("parallel", "arbitrary")),
        name="conv_ffn",
    )(*args)
    return (outs[0], outs[1]) if has_next else (outs[0], None)


def _head_norm_rope(x, gain, cr, sr, cc, sc):
    y = x * lax.rsqrt(jnp.mean(x * x, axis=0, keepdims=True) + EPS) * gain
    f = ROPE_F
    r1, r2, c1, c2 = y[0:f], y[f:2 * f], y[2 * f:3 * f], y[3 * f:4 * f]
    return jnp.concatenate([r1 * cr - r2 * sr, r2 * cr + r1 * sr,
                            c1 * cc - c2 * sc, c2 * cc + c1 * sc], axis=0)


def _qkv_kernel(h_ref, wt_ref, gq_ref, gk_ref, cr_ref, sr_ref, cc_ref, sc_ref,
                q_ref, k_ref, v_ref, *, tn, q_scale):
    h = h_ref[...]
    rope = (cr_ref[...], sr_ref[...], cc_ref[...], sc_ref[...])
    grp = GQA_GROUP * HEAD_DIM
    nt_dims = (((1,), (1,)), ((), ()))

    def proj(row0):
        return lax.dot_general(wt_ref[row0:row0 + grp, :], h, nt_dims,
                               preferred_element_type=F32)

    gq = gq_ref[...]
    for kv in range(N_KV_HEADS):
        r = proj(kv * grp)
        for g in range(GQA_GROUP):
            y = _head_norm_rope(r[g * HEAD_DIM:(g + 1) * HEAD_DIM], gq, *rope) * q_scale
            y = y.astype(BF16)
            for j in range(tn // ATTN_TQ):
                col = (j * GQA_GROUP + g) * ATTN_TQ
                q_ref[kv * HEAD_DIM:(kv + 1) * HEAD_DIM, col:col + ATTN_TQ] = (
                    y[:, j * ATTN_TQ:(j + 1) * ATTN_TQ])
    gk = gk_ref[...]
    r = proj(N_HEADS * HEAD_DIM)
    for kv in range(N_KV_HEADS):
        y = _head_norm_rope(r[kv * HEAD_DIM:(kv + 1) * HEAD_DIM], gk, *rope)
        k_ref[:, kv * HEAD_DIM:(kv + 1) * HEAD_DIM] = y.T.astype(BF16)
    r = proj((N_HEADS + N_KV_HEADS) * HEAD_DIM)
    ones = jnp.ones((SUBLANES_BF16, ATTN_TK), BF16)
    for kv in range(N_KV_HEADS):
        y = r[kv * HEAD_DIM:(kv + 1) * HEAD_DIM].astype(BF16)
        for b in range(tn // ATTN_TK):
            v_ref[kv, b, 0:HEAD_DIM, :] = y[:, b * ATTN_TK:(b + 1) * ATTN_TK]
            v_ref[kv, b, HEAD_DIM:, :] = ones


def qkv_project(h, w_t, layer, q_gain, k_gain, rope_t, seq_len):
    t, d = h.shape
    tn = QKV_TILE
    kb = tn // ATTN_TK
    per_seq = seq_len // tn
    q_scale = HEAD_DIM ** -0.5 * math.log2(math.e)
    gq = jnp.broadcast_to(q_gain.reshape(HEAD_DIM, 1), (HEAD_DIM, tn))
    gk = jnp.broadcast_to(k_gain.reshape(HEAD_DIM, 1), (HEAD_DIM, tn))
    rope_spec = pl.BlockSpec((ROPE_F, tn), lambda i: (0, i % per_seq))
    gain_spec = pl.BlockSpec((HEAD_DIM, tn), lambda i: (0, 0))
    return pl.pallas_call(
        functools.partial(_qkv_kernel, tn=tn, q_scale=q_scale),
        grid=(t // tn,),
        in_specs=[pl.BlockSpec((tn, d), lambda i: (i, 0)),
                  _resident((None,) + w_t.shape[1:], lambda i: (layer, 0, 0)),
                  gain_spec, gain_spec, rope_spec, rope_spec, rope_spec, rope_spec],
        out_specs=[pl.BlockSpec((N_KV_HEADS * HEAD_DIM, GQA_GROUP * tn), lambda i: (0, i)),
                   pl.BlockSpec((tn, N_KV_HEADS * HEAD_DIM), lambda i: (i, 0)),
                   pl.BlockSpec((N_KV_HEADS, kb, HEAD_DIM + SUBLANES_BF16, ATTN_TK),
                                lambda i: (0, i, 0, 0))],
        out_shape=[jax.ShapeDtypeStruct((N_KV_HEADS * HEAD_DIM, GQA_GROUP * t), BF16),
                   jax.ShapeDtypeStruct((t, N_KV_HEADS * HEAD_DIM), BF16),
                   jax.ShapeDtypeStruct((N_KV_HEADS, t // ATTN_TK, HEAD_DIM + SUBLANES_BF16,
                                         ATTN_TK), BF16)],
        compiler_params=_params(("parallel",)),
        name="qkv_project",
    )(h, w_t, gq, gk, *rope_t)


def _attn_online(q, k_ref, v_ref, m_ref, acc_ref, s_ref, mb_ref, p_ref, al_ref, n_kblocks):
    last_slot = ATTN_SLOTS - 1

    def scores(start, slot):
        s = _dot(k_ref[pl.ds(start, ATTN_TK), :], q)
        s_ref[slot] = s
        mb_ref[slot] = jnp.max(s, axis=0, keepdims=True)

    m_ref[...] = jnp.full_like(m_ref, -jnp.inf)
    acc_ref[...] = jnp.zeros_like(acc_ref)
    p_ref[last_slot] = jnp.zeros(p_ref.shape[1:], p_ref.dtype)
    al_ref[last_slot] = jnp.ones(al_ref.shape[1:], al_ref.dtype)
    scores(0, 0)

    def step(c, slot):
        nxt, prv = (slot + 1) % ATTN_SLOTS, (slot - 1) % ATTN_SLOTS
        c_next = jnp.minimum(c + 1, n_kblocks - 1)
        scores(pl.multiple_of(c_next * ATTN_TK, ATTN_TK), nxt)
        c_prev = jnp.maximum(c - 1, 0)
        acc_ref[...] = acc_ref[...] * al_ref[prv] + _dot(v_ref[0, c_prev], p_ref[prv])
        m_old = m_ref[...]
        m_new = jnp.maximum(m_old, mb_ref[slot])
        p_ref[slot] = jnp.exp2(s_ref[slot] - m_new).astype(p_ref.dtype)
        al_ref[slot] = jnp.exp2(m_old - m_new)
        m_ref[...] = m_new

    def body(i, carry):
        for j in range(ATTN_SLOTS):
            step(ATTN_SLOTS * i + j, j)
        return carry

    lax.fori_loop(0, n_kblocks // ATTN_SLOTS, body, 0)
    acc_ref[...] = (acc_ref[...] * al_ref[last_slot]
                    + _dot(v_ref[0, n_kblocks - 1], p_ref[last_slot]))


def _attn_fixed_shift(q, k_ref, v_ref, acc_ref, shift, n_kblocks):
    acc_ref[...] = jnp.zeros_like(acc_ref)

    def body(i, carry):
        acc = acc_ref[...]
        for j in range(ATTN_SLOTS):
            c = ATTN_SLOTS * i + j
            start = pl.multiple_of(c * ATTN_TK, ATTN_TK)
            s = _dot(k_ref[pl.ds(start, ATTN_TK), :], q)
            acc = acc + _dot(v_ref[0, c], jnp.exp2(s - shift).astype(BF16))
        acc_ref[...] = acc
        return carry

    lax.fori_loop(0, n_kblocks // ATTN_SLOTS, body, 0)


def _attn_kernel(bound_ref, q_ref, k_ref, v_ref, o_ref, m_ref, acc_ref, s_ref, mb_ref, p_ref,
                 al_ref, *, n_kblocks):
    q = q_ref[...]
    bound = bound_ref[0, 0]

    small = bound <= ATTN_MAX_FIXED_SHIFT

    @pl.when(small)
    def _():
        _attn_fixed_shift(q, k_ref, v_ref, acc_ref, bound, n_kblocks)

    @pl.when(jnp.logical_not(small))
    def _():
        _attn_online(q, k_ref, v_ref, m_ref, acc_ref, s_ref, mb_ref, p_ref, al_ref, n_kblocks)

    acc = acc_ref[...]
    out = acc[0:HEAD_DIM] / acc[HEAD_DIM:HEAD_DIM + 1]
    for g in range(GQA_GROUP):
        o_ref[:, g * HEAD_DIM:(g + 1) * HEAD_DIM] = (
            out[:, g * ATTN_TQ:(g + 1) * ATTN_TQ].T.astype(o_ref.dtype))


def score_bound(q_gain, k_gain):
    b = (HEAD_DIM ** 0.5 * math.log2(math.e)) * jnp.max(jnp.abs(q_gain)) * jnp.max(jnp.abs(k_gain))
    return (b * ATTN_BOUND_SLACK).reshape(1, 1).astype(F32)


def flash_attention(q, k, v, bound, batch, seq_len):
    t = batch * seq_len
    q_per_seq = seq_len // ATTN_TQ
    n_kblocks = seq_len // ATTN_TK
    assert n_kblocks % ATTN_SLOTS == 0
    lanes = GQA_GROUP * ATTN_TQ
    return pl.pallas_call(
        functools.partial(_attn_kernel, n_kblocks=n_kblocks),
        grid=(batch, N_KV_HEADS, q_per_seq),
        in_specs=[pl.BlockSpec(memory_space=pltpu.SMEM),
                  pl.BlockSpec((HEAD_DIM, lanes), lambda b, kv, qi: (kv, b * q_per_seq + qi)),
                  pl.BlockSpec((seq_len, HEAD_DIM), lambda b, kv, qi: (b, kv)),
                  pl.BlockSpec((1, n_kblocks, HEAD_DIM + SUBLANES_BF16, ATTN_TK),
                               lambda b, kv, qi: (kv, b, 0, 0))],
        out_specs=pl.BlockSpec((ATTN_TQ, GQA_GROUP * HEAD_DIM),
                               lambda b, kv, qi: (b * q_per_seq + qi, kv)),
        out_shape=jax.ShapeDtypeStruct((t, N_HEADS * HEAD_DIM), BF16),
        scratch_shapes=[pltpu.VMEM((1, lanes), F32),
                        pltpu.VMEM((HEAD_DIM + SUBLANES_BF16, lanes), F32),
                        pltpu.VMEM((ATTN_SLOTS, ATTN_TK, lanes), F32),
                        pltpu.VMEM((ATTN_SLOTS, 1, lanes), F32),
                        pltpu.VMEM((ATTN_SLOTS, ATTN_TK, lanes), BF16),
                        pltpu.VMEM((ATTN_SLOTS, 1, lanes), F32)],
        compiler_params=_params(("parallel", "parallel", "parallel")),
        name="flash_attention",
    )(bound, q, k, v)


def _rope_tables_t(n):
    rows = n // GRID_W
    row = jnp.repeat(jnp.arange(rows, dtype=F32), GRID_W)
    col = jnp.tile(jnp.arange(GRID_W, dtype=F32), rows)
    inv_freq = ROPE_THETA ** (-jnp.arange(0, ROPE_AXIS_DIM, 2, dtype=F32) / ROPE_AXIS_DIM)
    ang_r = inv_freq[:, None] * row[None, :]
    ang_c = inv_freq[:, None] * col[None, :]
    return jnp.cos(ang_r), jnp.sin(ang_r), jnp.cos(ang_c), jnp.sin(ang_c)


def _dft_split(n):
    n2 = max(1, int(round(math.sqrt(n) / 4)))
    while n % n2:
        n2 -= 1
    return n // n2, n2


def _dft_tables(n):
    n1, n2 = _dft_split(n)
    k1 = np.arange(n1)
    ang1 = 2.0 * np.pi * ((k1[:, None] * k1[None, :]) % n1) / n1
    m1 = np.concatenate([np.cos(ang1), -np.sin(ang1)], axis=0)
    t2 = np.arange(n2)
    ang_tw = 2.0 * np.pi * (t2[:, None] * k1[None, :]) / n
    ang2 = 2.0 * np.pi * ((t2[:, None] * t2[None, :]) % n2) / n2
    eye = np.eye(DFT_INTERLEAVE)
    kc = np.einsum("kt,ij->kijt", np.cos(ang2), eye).reshape(n2 * DFT_INTERLEAVE, -1)
    ks = np.einsum("kt,ij->kijt", np.sin(ang2), eye).reshape(n2 * DFT_INTERLEAVE, -1)
    k_re = np.concatenate([kc, ks], axis=1)
    k_im = np.concatenate([-ks, kc], axis=1)
    c = np.arange(FNET_GROUP_DIM)
    ang_c = 2.0 * np.pi * ((c[:, None] * c[None, :]) % FNET_GROUP_DIM) / FNET_GROUP_DIM
    as_bf16 = lambda a: jnp.asarray(a, F32).astype(BF16)
    tw_shape = (n2, n1, LANES)
    return dict(
        n1=n1, n2=n2, m1=as_bf16(m1), k_re=as_bf16(k_re), k_im=as_bf16(k_im),
        cc=as_bf16(np.cos(ang_c)), sc=as_bf16(np.sin(ang_c)),
        tw_cos=jnp.broadcast_to(jnp.asarray(np.cos(ang_tw), F32)[:, :, None], tw_shape),
        tw_sin=jnp.broadcast_to(jnp.asarray(np.sin(ang_tw), F32)[:, :, None], tw_shape))


def _dft1_kernel(x_ref, m1_ref, twc_ref, tws_ref, o_ref, *, n1):
    r = _dot(m1_ref[...], x_ref[...])
    twc, tws = twc_ref[0], tws_ref[0]
    for j in range(x_ref.shape[1] // LANES):
        sl = slice(j * LANES, (j + 1) * LANES)
        ar, ai = r[0:n1, sl], r[n1:, sl]
        o_ref[0, 0, :, sl] = (ar * twc + ai * tws).astype(o_ref.dtype)
        o_ref[0, 1, :, sl] = (ai * twc - ar * tws).astype(o_ref.dtype)


def _dft2_kernel(a_ref, kre_ref, kim_ref, cc_ref, sc_ref, o_ref, *, rows, n2, scale):
    a = a_ref[0].reshape(2 * rows, a_ref.shape[-1])
    zr = _dot(kre_ref[...], a).astype(BF16)
    zi = _dot(kim_ref[...], a).astype(BF16)
    cc, sc = cc_ref[...], sc_ref[...]
    for g in range(FNET_GROUPS):
        sl = slice(g * FNET_GROUP_DIM, (g + 1) * FNET_GROUP_DIM)
        y = (_dot(zr[:, sl], cc) + _dot(zi[:, sl], sc)) * scale
        o_ref[0, :, :, sl] = y.reshape(n2, DFT_INTERLEAVE, FNET_GROUP_DIM)


def fourier_real_2d(h, batch, seq_len, tabs):
    t, d = h.shape
    n1, n2 = tabs["n1"], tabs["n2"]
    rows = n2 * DFT_INTERLEAVE
    a2 = pl.pallas_call(
        functools.partial(_dft1_kernel, n1=n1),
        grid=(batch, n2),
        in_specs=[pl.BlockSpec((n1, d), lambda b, j: (b, j)),
                  _resident((2 * n1, n1), lambda b, j: (0, 0)),
                  pl.BlockSpec((1, n1, LANES), lambda b, j: (j, 0, 0)),
                  pl.BlockSpec((1, n1, LANES), lambda b, j: (j, 0, 0))],
        out_specs=pl.BlockSpec((1, 2, n1, d), lambda b, j: (b, 0, 0, j)),
        out_shape=jax.ShapeDtypeStruct((batch, 2, n1, n2 * d), BF16),
        compiler_params=_params(("parallel", "parallel")),
        name="dft_stage1",
    )(h.reshape(batch * n1, n2 * d), tabs["m1"], tabs["tw_cos"], tabs["tw_sin"])
    scale = 1.0 / math.sqrt(seq_len * FNET_GROUP_DIM)
    y = pl.pallas_call(
        functools.partial(_dft2_kernel, rows=rows, n2=n2, scale=scale),
        grid=(batch, n1 // DFT_INTERLEAVE),
        in_specs=[pl.BlockSpec((1, 2, rows, d), lambda b, j: (b, 0, j, 0)),
                  _resident((rows, 2 * rows), lambda b, j: (0, 0)),
                  _resident((rows, 2 * rows), lambda b, j: (0, 0)),
                  _resident((FNET_GROUP_DIM, FNET_GROUP_DIM), lambda b, j: (0, 0)),
                  _resident((FNET_GROUP_DIM, FNET_GROUP_DIM), lambda b, j: (0, 0))],
        out_specs=pl.BlockSpec((1, n2, DFT_INTERLEAVE, d), lambda b, j: (b, 0, j, 0)),
        out_shape=jax.ShapeDtypeStruct((batch, n2, n1, d), F32),
        compiler_params=_params(("parallel", "parallel")),
        name="dft_stage2",
    )(a2.reshape(batch, 2, n1 * n2, d), tabs["k_re"], tabs["k_im"], tabs["cc"], tabs["sc"])
    return y.reshape(t, d)


def _conf_kernel(hp_ref, hm_ref, hn_ref, w1_ref, b1_ref, dw_ref, db_ref, lg_ref, lb_ref,
                 o_ref, hh_ref, y_ref, sh_ref, *, tm, tiles_per_seq):
    d = o_ref.shape[1]
    i = pl.program_id(0)
    first, last = _seq_edges(i, tiles_per_seq)
    hh_ref[0:HALO, :] = hp_ref[...]
    hh_ref[HALO:HALO + tm, :] = hm_ref[...]
    hh_ref[HALO + tm:, :] = hn_ref[...]
    a = hh_ref[...]
    ext = tm + 2 * HALO
    rows = lax.broadcasted_iota(jnp.int32, (ext, 1), 0)
    lo = jnp.where(first, HALO, 0)
    hi = jnp.where(last, HALO + tm, ext)
    inside = (rows >= lo) & (rows < hi)
    span = tm + SUBLANES_F32 * (2 * HALO // SUBLANES_F32 - 1)
    for j in range(d // CONF_LANE_CHUNK):
        sl = slice(j * CONF_LANE_CHUNK, (j + 1) * CONF_LANE_CHUNK)
        sg = slice(d + j * CONF_LANE_CHUNK, d + (j + 1) * CONF_LANE_CHUNK)
        za = _dot(a, w1_ref[:, sl]) + b1_ref[:, sl]
        zg = _dot(a, w1_ref[:, sg]) + b1_ref[:, sg]
        u = jnp.where(inside, za * _sigmoid(zg), 0.0)
        for b in range(SUBLANES_F32):
            sh_ref[b] = u[b:b + span]
        sub = CONF_ROW_BLOCK // SUBLANES_F32
        for cc in range(CONF_LANE_CHUNK // LANES):
            lanes = slice(cc * LANES, (cc + 1) * LANES)
            col = slice(j * CONF_LANE_CHUNK + cc * LANES, j * CONF_LANE_CHUNK + (cc + 1) * LANES)
            taps = [jnp.broadcast_to(dw_ref[k:k + 1, col], (SUBLANES_F32, LANES))
                    for k in range(CONF_KERNEL)]
            bias = jnp.broadcast_to(db_ref[:, col], (SUBLANES_F32, LANES))
            for r0 in range(0, tm, CONF_ROW_BLOCK):
                acc = jnp.broadcast_to(bias[None], (sub, SUBLANES_F32, LANES))
                for k in range(CONF_KERNEL):
                    off = HALO - CONF_PAD + k
                    base = (off // SUBLANES_F32) * SUBLANES_F32 + r0
                    win = sh_ref[off % SUBLANES_F32, base:base + CONF_ROW_BLOCK, lanes]
                    acc = acc + taps[k][None] * win.reshape(sub, SUBLANES_F32, LANES)
                y_ref[r0:r0 + CONF_ROW_BLOCK, col] = acc.reshape(CONF_ROW_BLOCK, LANES)
    y = y_ref[...]
    mu = jnp.mean(y, axis=-1, keepdims=True)
    yc = y - mu
    yn = yc * lax.rsqrt(jnp.mean(yc * yc, axis=-1, keepdims=True) + EPS) * lg_ref[...] + lb_ref[...]
    o_ref[...] = (yn * _sigmoid(yn)).astype(o_ref.dtype)


def conformer_inner(h, w1, layer, b1, dw_w, dw_b, ln_g, ln_b, seq_len):
    t, d = h.shape
    tm = CONF_ROW_TILE
    n_tiles = t // tm
    fixed = lambda i: (0, 0)
    return pl.pallas_call(
        functools.partial(_conf_kernel, tm=tm, tiles_per_seq=seq_len // tm),
        grid=(n_tiles,),
        in_specs=_halo_specs(tm, d, n_tiles) + [
            _resident((None, d, 2 * d), lambda i: (layer, 0, 0)), pl.BlockSpec((1, 2 * d), fixed),
            pl.BlockSpec((CONF_KERNEL, d), fixed), pl.BlockSpec((1, d), fixed),
            pl.BlockSpec((1, d), fixed), pl.BlockSpec((1, d), fixed)],
        out_specs=pl.BlockSpec((tm, d), lambda i: (i, 0)),
        out_shape=jax.ShapeDtypeStruct((t, d), BF16),
        scratch_shapes=[pltpu.VMEM((tm + 2 * HALO, d), BF16), pltpu.VMEM((tm, d), F32),
                        pltpu.VMEM((SUBLANES_F32, tm + 2 * HALO - SUBLANES_F32, CONF_LANE_CHUNK),
                                   F32)],
        compiler_params=_params(("parallel",)),
        name="conformer_inner",
    )(h, h, h, w1, b1.reshape(1, 2 * d), dw_w, dw_b.reshape(1, d),
      ln_g.reshape(1, d), ln_b.reshape(1, d))


def _run_trunk(x, p):
    batch, seq_len, d = x.shape
    t = batch * seq_len
    xf = x.reshape(t, d)
    rope_t = _rope_tables_t(seq_len)
    tabs = _dft_tables(seq_len)
    h = rmsnorm_bf16(xf, p["norm_mix_pre"][0])
    for i in range(DEPTH):
        j, kind = i // N_MIXERS, i % N_MIXERS
        g_post, g_ffn = p["norm_mix_post"][i], p["norm_ffn_pre"][i]
        if kind == 0:
            q, k, v = qkv_project(h, p["attn_w_qkv_t"], j, p["attn_q_gain"][j],
                                  p["attn_k_gain"][j], rope_t, seq_len)
            bound = score_bound(p["attn_q_gain"][j], p["attn_k_gain"][j])
            o = flash_attention(q, k, v, bound, batch, seq_len)
            xf, h = proj_residual(o, p["attn_w_o"], j, None, xf, g_post, g_ffn)
        elif kind == 1:
            y = fourier_real_2d(h, batch, seq_len, tabs)
            xf, h = proj_residual(y, p["fnet_w_out"], j, p["fnet_b_out"][j], xf, g_post, g_ffn)
        else:
            u = conformer_inner(h, p["conv_w_pw1"], j, p["conv_b_pw1"][j], p["conv_dw_w"][j],
                                p["conv_dw_b"][j], p["conv_ln_g"][j], p["conv_ln_b"][j], seq_len)
            xf, h = proj_residual(u, p["conv_w_pw2"], j, p["conv_b_pw2"][j], xf, g_post, g_ffn)
        g_next = p["norm_mix_pre"][i + 1] if i + 1 < DEPTH else None
        xf, h = conv_ffn_residual(h, p["ffn_w_up"], p["ffn_dw_w"], p["ffn_dw_b"],
                                  p["ffn_w_down"], i, xf, p["norm_ffn_post"][i], g_next, seq_len)
    return xf.reshape(batch, seq_len, d)


def kernel(x_prompt, x_sample, norm_mix_pre, norm_mix_post, norm_ffn_pre, norm_ffn_post, attn_w_qkv, attn_q_gain, attn_k_gain, attn_w_o, fnet_w_out, fnet_b_out, conv_w_pw1, conv_b_pw1, conv_dw_w, conv_dw_b, conv_ln_g, conv_ln_b, conv_w_pw2, conv_b_pw2, ffn_w_up, ffn_dw_w, ffn_dw_b, ffn_w_down):
    p = {
        "norm_mix_pre": norm_mix_pre, "norm_mix_post": norm_mix_post,
        "norm_ffn_pre": norm_ffn_pre, "norm_ffn_post": norm_ffn_post,
        "attn_w_qkv_t": jnp.swapaxes(attn_w_qkv, 1, 2).astype(BF16),
        "attn_q_gain": attn_q_gain, "attn_k_gain": attn_k_gain,
        "attn_w_o": attn_w_o.astype(BF16),
        "fnet_w_out": fnet_w_out.astype(BF16), "fnet_b_out": fnet_b_out,
        "conv_w_pw1": conv_w_pw1.astype(BF16), "conv_b_pw1": conv_b_pw1,
        "conv_dw_w": conv_dw_w, "conv_dw_b": conv_dw_b,
        "conv_ln_g": conv_ln_g, "conv_ln_b": conv_ln_b,
        "conv_w_pw2": conv_w_pw2.astype(BF16), "conv_b_pw2": conv_b_pw2,
        "ffn_w_up": ffn_w_up.astype(BF16), "ffn_dw_w": ffn_dw_w, "ffn_dw_b": ffn_dw_b,
        "ffn_w_down": ffn_w_down.astype(BF16),
    }
    return (_run_trunk(x_prompt, p), _run_trunk(x_sample, p))
```

```python
import functools
import math

import numpy as np
import jax
import jax.numpy as jnp
from jax import lax
from jax.experimental import pallas as pl
from jax.experimental.pallas import tpu as pltpu

F32 = jnp.float32
BF16 = jnp.bfloat16

D_MODEL = 2048
DEPTH = 4
N_MIXERS = 3
GRID_W = 64
HEAD_DIM = 128
N_HEADS = D_MODEL // HEAD_DIM
N_KV_HEADS = N_HEADS // 4
GQA_GROUP = N_HEADS // N_KV_HEADS
ROPE_THETA = 10000.0
ROPE_AXIS_DIM = HEAD_DIM // 2
ROPE_F = ROPE_AXIS_DIM // 2
FNET_GROUPS = 4
FNET_GROUP_DIM = D_MODEL // FNET_GROUPS
CONF_KERNEL = 31
CONF_PAD = (CONF_KERNEL - 1) // 2
FFN_DIM = 4 * D_MODEL
EPS = 1e-6

V7X_VMEM_BYTES = 64 * 1024 * 1024
V7X_VMEM_LIMIT = V7X_VMEM_BYTES - 4 * 1024 * 1024
LANES = 128
SUBLANES_F32 = 8
SUBLANES_BF16 = 16
HALO = SUBLANES_BF16

ROW_TILE = 512
FFN_CHUNK = 1024
FFN_SUBCHUNK = 256
CONF_ROW_TILE = 512
CONF_LANE_CHUNK = 512
CONF_ROW_BLOCK = 32
QKV_TILE = 1024
ATTN_TQ = 512
ATTN_TK = 512
ATTN_SLOTS = 4
ATTN_MAX_FIXED_SHIFT = 60.0
ATTN_BOUND_SLACK = 1.02
DFT_INTERLEAVE = SUBLANES_F32


def _params(semantics):
    return pltpu.CompilerParams(dimension_semantics=semantics,
                                vmem_limit_bytes=V7X_VMEM_LIMIT)


def _resident(shape, index_map):
    return pl.BlockSpec(shape, index_map, pipeline_mode=pl.Buffered(1))


def _rms(x, g):
    return x * lax.rsqrt(jnp.mean(x * x, axis=-1, keepdims=True) + EPS) * g


def _sigmoid(x):
    return 1.0 / (1.0 + jnp.exp(-x))


def _gelu_tanh(x):
    a = -2.0 * math.sqrt(2.0 / math.pi) * math.log2(math.e)
    return x / (1.0 + jnp.exp2(x * (a + (a * 0.044715) * (x * x))))


def _dot(a, b):
    return jnp.dot(a, b, preferred_element_type=F32)


def _store_sandwich(m, x_ref, gp_ref, gn_ref, o_ref, h_ref):
    xn = x_ref[...] + _rms(m, gp_ref[...])
    o_ref[...] = xn
    if h_ref is not None:
        h_ref[...] = _rms(xn, gn_ref[...]).astype(BF16)


def _rmsnorm_kernel(x_ref, g_ref, o_ref):
    o_ref[...] = _rms(x_ref[...], g_ref[...]).astype(o_ref.dtype)


def rmsnorm_bf16(x, g):
    t, d = x.shape
    return pl.pallas_call(
        _rmsnorm_kernel,
        grid=(t // ROW_TILE,),
        in_specs=[pl.BlockSpec((ROW_TILE, d), lambda i: (i, 0)),
                  pl.BlockSpec((1, d), lambda i: (0, 0))],
        out_specs=pl.BlockSpec((ROW_TILE, d), lambda i: (i, 0)),
        out_shape=jax.ShapeDtypeStruct((t, d), BF16),
        compiler_params=_params(("parallel",)),
        name="rmsnorm",
    )(x, g.reshape(1, d))


def _proj_res_kernel(*refs, has_bias, has_next):
    a_ref, w_ref = refs[0], refs[1]
    pos = 2
    b_ref = None
    if has_bias:
        b_ref = refs[pos]
        pos += 1
    x_ref, gp_ref = refs[pos], refs[pos + 1]
    pos += 2
    gn_ref = None
    if has_next:
        gn_ref = refs[pos]
        pos += 1
    o_ref = refs[pos]
    h_ref = refs[pos + 1] if has_next else None
    m = _dot(a_ref[...].astype(BF16), w_ref[...])
    if has_bias:
        m = m + b_ref[...]
    _store_sandwich(m, x_ref, gp_ref, gn_ref, o_ref, h_ref)


def proj_residual(a, w, layer, bias, x, g_post, g_next):
    t, k = a.shape
    d = w.shape[2]
    has_bias = bias is not None
    has_next = g_next is not None
    row = lambda i: (i, 0)
    fixed = lambda i: (0, 0)
    in_specs = [pl.BlockSpec((ROW_TILE, k), row),
                _resident((None, k, d), lambda i: (layer, 0, 0))]
    args = [a, w]
    if has_bias:
        in_specs.append(pl.BlockSpec((1, d), fixed))
        args.append(bias.reshape(1, d))
    in_specs += [pl.BlockSpec((ROW_TILE, d), row), pl.BlockSpec((1, d), fixed)]
    args += [x, g_post.reshape(1, d)]
    out_specs = [pl.BlockSpec((ROW_TILE, d), row)]
    out_shape = [jax.ShapeDtypeStruct((t, d), F32)]
    if has_next:
        in_specs.append(pl.BlockSpec((1, d), fixed))
        args.append(g_next.reshape(1, d))
        out_specs.append(pl.BlockSpec((ROW_TILE, d), row))
        out_shape.append(jax.ShapeDtypeStruct((t, d), BF16))
    outs = pl.pallas_call(
        functools.partial(_proj_res_kernel, has_bias=has_bias, has_next=has_next),
        grid=(t // ROW_TILE,),
        in_specs=in_specs, out_specs=out_specs, out_shape=out_shape,
        compiler_params=_params(("parallel",)),
        name="proj_residual",
    )(*args)
    return (outs[0], outs[1]) if has_next else (outs[0], None)


def _halo_specs(tm, d, n_tiles):
    per = tm // HALO
    last = n_tiles * per - 1
    return [pl.BlockSpec((HALO, d), lambda i, *_: (jnp.maximum(i * per - 1, 0), 0)),
            pl.BlockSpec((tm, d), lambda i, *_: (i, 0)),
            pl.BlockSpec((HALO, d), lambda i, *_: (jnp.minimum((i + 1) * per, last), 0))]


def _seq_edges(i, tiles_per_seq):
    pos = i % tiles_per_seq
    return pos == 0, pos == tiles_per_seq - 1


def _ffn_kernel(*refs, tm, tc, tiles_per_seq, n_chunks, has_next):
    (hp_ref, hm_ref, hn_ref, wg_ref, wv_ref, cg_ref, cv_ref, bg_ref, bv_ref,
     wd_ref, x_ref, gp_ref) = refs[:12]
    pos = 12
    gn_ref = None
    if has_next:
        gn_ref = refs[pos]
        pos += 1
    o_ref = refs[pos]
    pos += 1
    h_ref = None
    if has_next:
        h_ref = refs[pos]
        pos += 1
    hh_ref, acc_ref = refs[pos], refs[pos + 1]

    i = pl.program_id(0)
    c = pl.program_id(1)

    @pl.when(c == 0)
    def _():
        first, last = _seq_edges(i, tiles_per_seq)
        keep_prev = jnp.where(first, 0.0, 1.0)
        keep_next = jnp.where(last, 0.0, 1.0)
        hh_ref[0:tm, :] = hm_ref[...]
        r = lax.broadcasted_iota(jnp.int32, (HALO, 1), 0)
        prev_row = hp_ref[...].astype(F32)[HALO - 1:HALO, :] * keep_prev
        next_row = hn_ref[...].astype(F32)[0:1, :] * keep_next
        hh_ref[tm:, :] = jnp.where(r == 0, prev_row, jnp.where(r == 1, next_row, 0.0)
                                   ).astype(BF16)
        acc_ref[...] = jnp.zeros_like(acc_ref)

    half = tm // 2
    row_id = lax.broadcasted_iota(jnp.int32, (tm, 1), 0)

    def up_conv(w_ref, c_ref, b_ref, sl):
        w_up = w_ref[:, sl]
        v_lo = _dot(hh_ref[0:half, :], w_up)
        v_hi = _dot(hh_ref[half:, :], w_up)
        v = jnp.concatenate([v_lo, v_hi[0:half]], axis=0)
        before = jnp.where(row_id == 0, v_hi[half:half + 1], pltpu.roll(v, 1, 0))
        after = jnp.where(row_id == tm - 1, v_hi[half + 1:half + 2], pltpu.roll(v, tm - 1, 0))
        w = c_ref[:, sl]
        return w[0:1, :] * before + w[1:2, :] * v + w[2:3, :] * after + b_ref[:, sl]

    subs = [slice(s * FFN_SUBCHUNK, (s + 1) * FFN_SUBCHUNK) for s in range(tc // FFN_SUBCHUNK)]
    acts = []
    for sl in subs:
        ug = up_conv(wg_ref, cg_ref, bg_ref, sl)
        uv = up_conv(wv_ref, cv_ref, bv_ref, sl)
        acts.append((_gelu_tanh(ug) * uv).astype(BF16))
    for sl, act in zip(subs, acts):
        acc_ref[...] += _dot(act, wd_ref[sl, :])

    @pl.when(c == n_chunks - 1)
    def _():
        _store_sandwich(acc_ref[...], x_ref, gp_ref, gn_ref, o_ref, h_ref)


def conv_ffn_residual(h, w_up, dw_w, dw_b, w_down, layer, x, g_post, g_next, seq_len):
    t, d = h.shape
    f = w_down.shape[1]
    tm, tc = ROW_TILE, FFN_CHUNK
    n_tiles, n_chunks = t // tm, f // tc
    has_next = g_next is not None
    row = lambda i, c: (i, 0)
    fixed = lambda i, c: (0, 0)
    in_specs = _halo_specs(tm, d, n_tiles) + [
        pl.BlockSpec((None, d, tc), lambda i, c: (layer, 0, c)),
        pl.BlockSpec((None, d, tc), lambda i, c: (layer, 0, c + n_chunks)),
        pl.BlockSpec((None, 3, tc), lambda i, c: (layer, 0, c)),
        pl.BlockSpec((None, 3, tc), lambda i, c: (layer, 0, c + n_chunks)),
        pl.BlockSpec((None, 1, tc), lambda i, c: (layer, 0, c)),
        pl.BlockSpec((None, 1, tc), lambda i, c: (layer, 0, c + n_chunks)),
        pl.BlockSpec((None, tc, d), lambda i, c: (layer, c, 0)),
        pl.BlockSpec((tm, d), row),
        pl.BlockSpec((1, d), fixed),
    ]
    dw_b3 = dw_b.reshape(dw_b.shape[0], 1, 2 * f)
    args = [h, h, h, w_up, w_up, dw_w, dw_w, dw_b3, dw_b3, w_down, x, g_post.reshape(1, d)]
    out_specs = [pl.BlockSpec((tm, d), row)]
    out_shape = [jax.ShapeDtypeStruct((t, d), F32)]
    if has_next:
        in_specs.append(pl.BlockSpec((1, d), fixed))
        args.append(g_next.reshape(1, d))
        out_specs.append(pl.BlockSpec((tm, d), row))
        out_shape.append(jax.ShapeDtypeStruct((t, d), BF16))
    outs = pl.pallas_call(
        functools.partial(_ffn_kernel, tm=tm, tc=tc, tiles_per_seq=seq_len // tm,
                          n_chunks=n_chunks, has_next=has_next),
        grid=(n_tiles, n_chunks),
        in_specs=in_specs, out_specs=out_specs, out_shape=out_shape,
        scratch_shapes=[pltpu.VMEM((tm + HALO, d), BF16), pltpu.VMEM((tm, d), F32)],
        compiler_params=_params(("parallel", "arbitrary")),
        name="conv_ffn",
    )(*args)
    return (outs[0], outs[1]) if has_next else (outs[0], None)


def _head_norm_rope(x, gain, cr, sr, cc, sc):
    y = x * lax.rsqrt(jnp.mean(x * x, axis=0, keepdims=True) + EPS) * gain
    f = ROPE_F
    r1, r2, c1, c2 = y[0:f], y[f:2 * f], y[2 * f:3 * f], y[3 * f:4 * f]
    return jnp.concatenate([r1 * cr - r2 * sr, r2 * cr + r1 * sr,
                            c1 * cc - c2 * sc, c2 * cc + c1 * sc], axis=0)


def _qkv_kernel(h_ref, wt_ref, gq_ref, gk_ref, cr_ref, sr_ref, cc_ref, sc_ref,
                q_ref, k_ref, v_ref, *, tn, q_scale):
    h = h_ref[...]
    rope = (cr_ref[...], sr_ref[...], cc_ref[...], sc_ref[...])
    grp = GQA_GROUP * HEAD_DIM
    nt_dims = (((1,), (1,)), ((), ()))

    def proj(row0):
        return lax.dot_general(wt_ref[row0:row0 + grp, :], h, nt_dims,
                               preferred_element_type=F32)

    gq = gq_ref[...]
    for kv in range(N_KV_HEADS):
        r = proj(kv * grp)
        for g in range(GQA_GROUP):
            y = _head_norm_rope(r[g * HEAD_DIM:(g + 1) * HEAD_DIM], gq, *rope) * q_scale
            y = y.astype(BF16)
            for j in range(tn // ATTN_TQ):
                col = (j * GQA_GROUP + g) * ATTN_TQ
                q_ref[kv * HEAD_DIM:(kv + 1) * HEAD_DIM, col:col + ATTN_TQ] = (
                    y[:, j * ATTN_TQ:(j + 1) * ATTN_TQ])
    gk = gk_ref[...]
    r = proj(N_HEADS * HEAD_DIM)
    for kv in range(N_KV_HEADS):
        y = _head_norm_rope(r[kv * HEAD_DIM:(kv + 1) * HEAD_DIM], gk, *rope)
        k_ref[:, kv * HEAD_DIM:(kv + 1) * HEAD_DIM] = y.T.astype(BF16)
    r = proj((N_HEADS + N_KV_HEADS) * HEAD_DIM)
    ones = jnp.ones((SUBLANES_BF16, ATTN_TK), BF16)
    for kv in range(N_KV_HEADS):
        y = r[kv * HEAD_DIM:(kv + 1) * HEAD_DIM].astype(BF16)
        for b in range(tn // ATTN_TK):
            v_ref[kv, b, 0:HEAD_DIM, :] = y[:, b * ATTN_TK:(b + 1) * ATTN_TK]
            v_ref[kv, b, HEAD_DIM:, :] = ones


def qkv_project(h, w_t, layer, q_gain, k_gain, rope_t, seq_len):
    t, d = h.shape
    tn = QKV_TILE
    kb = tn // ATTN_TK
    per_seq = seq_len // tn
    q_scale = HEAD_DIM ** -0.5 * math.log2(math.e)
    gq = jnp.broadcast_to(q_gain.reshape(HEAD_DIM, 1), (HEAD_DIM, tn))
    gk = jnp.broadcast_to(k_gain.reshape(HEAD_DIM, 1), (HEAD_DIM, tn))
    rope_spec = pl.BlockSpec((ROPE_F, tn), lambda i: (0, i % per_seq))
    gain_spec = pl.BlockSpec((HEAD_DIM, tn), lambda i: (0, 0))
    return pl.pallas_call(
        functools.partial(_qkv_kernel, tn=tn, q_scale=q_scale),
        grid=(t // tn,),
        in_specs=[pl.BlockSpec((tn, d), lambda i: (i, 0)),
                  _resident((None,) + w_t.shape[1:], lambda i: (layer, 0, 0)),
                  gain_spec, gain_spec, rope_spec, rope_spec, rope_spec, rope_spec],
        out_specs=[pl.BlockSpec((N_KV_HEADS * HEAD_DIM, GQA_GROUP * tn), lambda i: (0, i)),
                   pl.BlockSpec((tn, N_KV_HEADS * HEAD_DIM), lambda i: (i, 0)),
                   pl.BlockSpec((N_KV_HEADS, kb, HEAD_DIM + SUBLANES_BF16, ATTN_TK),
                                lambda i: (0, i, 0, 0))],
        out_shape=[jax.ShapeDtypeStruct((N_KV_HEADS * HEAD_DIM, GQA_GROUP * t), BF16),
                   jax.ShapeDtypeStruct((t, N_KV_HEADS * HEAD_DIM), BF16),
                   jax.ShapeDtypeStruct((N_KV_HEADS, t // ATTN_TK, HEAD_DIM + SUBLANES_BF16,
                                         ATTN_TK), BF16)],
        compiler_params=_params(("parallel",)),
        name="qkv_project",
    )(h, w_t, gq, gk, *rope_t)


def _attn_online(q, k_ref, v_ref, m_ref, acc_ref, s_ref, mb_ref, p_ref, al_ref, n_kblocks):
    last_slot = ATTN_SLOTS - 1

    def scores(start, slot):
        s = _dot(k_ref[pl.ds(start, ATTN_TK), :], q)
        s_ref[slot] = s
        mb_ref[slot] = jnp.max(s, axis=0, keepdims=True)

    m_ref[...] = jnp.full_like(m_ref, -jnp.inf)
    acc_ref[...] = jnp.zeros_like(acc_ref)
    p_ref[last_slot] = jnp.zeros(p_ref.shape[1:], p_ref.dtype)
    al_ref[last_slot] = jnp.ones(al_ref.shape[1:], al_ref.dtype)
    scores(0, 0)

    def step(c, slot):
        nxt, prv = (slot + 1) % ATTN_SLOTS, (slot - 1) % ATTN_SLOTS
        c_next = jnp.minimum(c + 1, n_kblocks - 1)
        scores(pl.multiple_of(c_next * ATTN_TK, ATTN_TK), nxt)
        c_prev = jnp.maximum(c - 1, 0)
        acc_ref[...] = acc_ref[...] * al_ref[prv] + _dot(v_ref[0, c_prev], p_ref[prv])
        m_old = m_ref[...]
        m_new = jnp.maximum(m_old, mb_ref[slot])
        p_ref[slot] = jnp.exp2(s_ref[slot] - m_new).astype(p_ref.dtype)
        al_ref[slot] = jnp.exp2(m_old - m_new)
        m_ref[...] = m_new

    def body(i, carry):
        for j in range(ATTN_SLOTS):
            step(ATTN_SLOTS * i + j, j)
        return carry

    lax.fori_loop(0, n_kblocks // ATTN_SLOTS, body, 0)
    acc_ref[...] = (acc_ref[...] * al_ref[last_slot]
                    + _dot(v_ref[0, n_kblocks - 1], p_ref[last_slot]))


def _attn_fixed_shift(q, k_ref, v_ref, acc_ref, shift, n_kblocks):
    acc_ref[...] = jnp.zeros_like(acc_ref)

    def body(i, carry):
        acc = acc_ref[...]
        for j in range(ATTN_SLOTS):
            c = ATTN_SLOTS * i + j
            start = pl.multiple_of(c * ATTN_TK, ATTN_TK)
            s = _dot(k_ref[pl.ds(start, ATTN_TK), :], q)
            acc = acc + _dot(v_ref[0, c], jnp.exp2(s - shift).astype(BF16))
        acc_ref[...] = acc
        return carry

    lax.fori_loop(0, n_kblocks // ATTN_SLOTS, body, 0)


def _attn_kernel(bound_ref, q_ref, k_ref, v_ref, o_ref, m_ref, acc_ref, s_ref, mb_ref, p_ref,
                 al_ref, *, n_kblocks):
    q = q_ref[...]
    bound = bound_ref[0, 0]

    small = bound <= ATTN_MAX_FIXED_SHIFT

    @pl.when(small)
    def _():
        _attn_fixed_shift(q, k_ref, v_ref, acc_ref, bound, n_kblocks)

    @pl.when(jnp.logical_not(small))
    def _():
        _attn_online(q, k_ref, v_ref, m_ref, acc_ref, s_ref, mb_ref, p_ref, al_ref, n_kblocks)

    acc = acc_ref[...]
    out = acc[0:HEAD_DIM] / acc[HEAD_DIM:HEAD_DIM + 1]
    for g in range(GQA_GROUP):
        o_ref[:, g * HEAD_DIM:(g + 1) * HEAD_DIM] = (
            out[:, g * ATTN_TQ:(g + 1) * ATTN_TQ].T.astype(o_ref.dtype))


def score_bound(q_gain, k_gain):
    b = (HEAD_DIM ** 0.5 * math.log2(math.e)) * jnp.max(jnp.abs(q_gain)) * jnp.max(jnp.abs(k_gain))
    return (b * ATTN_BOUND_SLACK).reshape(1, 1).astype(F32)


def flash_attention(q, k, v, bound, batch, seq_len):
    t = batch * seq_len
    q_per_seq = seq_len // ATTN_TQ
    n_kblocks = seq_len // ATTN_TK
    assert n_kblocks % ATTN_SLOTS == 0
    lanes = GQA_GROUP * ATTN_TQ
    return pl.pallas_call(
        functools.partial(_attn_kernel, n_kblocks=n_kblocks),
        grid=(batch, N_KV_HEADS, q_per_seq),
        in_specs=[pl.BlockSpec(memory_space=pltpu.SMEM),
                  pl.BlockSpec((HEAD_DIM, lanes), lambda b, kv, qi: (kv, b * q_per_seq + qi)),
                  pl.BlockSpec((seq_len, HEAD_DIM), lambda b, kv, qi: (b, kv)),
                  pl.BlockSpec((1, n_kblocks, HEAD_DIM + SUBLANES_BF16, ATTN_TK),
                               lambda b, kv, qi: (kv, b, 0, 0))],
        out_specs=pl.BlockSpec((ATTN_TQ, GQA_GROUP * HEAD_DIM),
                               lambda b, kv, qi: (b * q_per_seq + qi, kv)),
        out_shape=jax.ShapeDtypeStruct((t, N_HEADS * HEAD_DIM), BF16),
        scratch_shapes=[pltpu.VMEM((1, lanes), F32),
                        pltpu.VMEM((HEAD_DIM + SUBLANES_BF16, lanes), F32),
                        pltpu.VMEM((ATTN_SLOTS, ATTN_TK, lanes), F32),
                        pltpu.VMEM((ATTN_SLOTS, 1, lanes), F32),
                        pltpu.VMEM((ATTN_SLOTS, ATTN_TK, lanes), BF16),
                        pltpu.VMEM((ATTN_SLOTS, 1, lanes), F32)],
        compiler_params=_params(("parallel", "parallel", "parallel")),
        name="flash_attention",
    )(bound, q, k, v)


def _rope_tables_t(n):
    rows = n // GRID_W
    row = jnp.repeat(jnp.arange(rows, dtype=F32), GRID_W)
    col = jnp.tile(jnp.arange(GRID_W, dtype=F32), rows)
    inv_freq = ROPE_THETA ** (-jnp.arange(0, ROPE_AXIS_DIM, 2, dtype=F32) / ROPE_AXIS_DIM)
    ang_r = inv_freq[:, None] * row[None, :]
    ang_c = inv_freq[:, None] * col[None, :]
    return jnp.cos(ang_r), jnp.sin(ang_r), jnp.cos(ang_c), jnp.sin(ang_c)


def _dft_split(n):
    n2 = max(1, int(round(math.sqrt(n) / 4)))
    while n % n2:
        n2 -= 1
    return n // n2, n2


def _dft_tables(n):
    n1, n2 = _dft_split(n)
    k1 = np.arange(n1)
    ang1 = 2.0 * np.pi * ((k1[:, None] * k1[None, :]) % n1) / n1
    m1 = np.concatenate([np.cos(ang1), -np.sin(ang1)], axis=0)
    t2 = np.arange(n2)
    ang_tw = 2.0 * np.pi * (t2[:, None] * k1[None, :]) / n
    ang2 = 2.0 * np.pi * ((t2[:, None] * t2[None, :]) % n2) / n2
    eye = np.eye(DFT_INTERLEAVE)
    kc = np.einsum("kt,ij->kijt", np.cos(ang2), eye).reshape(n2 * DFT_INTERLEAVE, -1)
    ks = np.einsum("kt,ij->kijt", np.sin(ang2), eye).reshape(n2 * DFT_INTERLEAVE, -1)
    k_re = np.concatenate([kc, ks], axis=1)
    k_im = np.concatenate([-ks, kc], axis=1)
    c = np.arange(FNET_GROUP_DIM)
    ang_c = 2.0 * np.pi * ((c[:, None] * c[None, :]) % FNET_GROUP_DIM) / FNET_GROUP_DIM
    as_bf16 = lambda a: jnp.asarray(a, F32).astype(BF16)
    tw_shape = (n2, n1, LANES)
    return dict(
        n1=n1, n2=n2, m1=as_bf16(m1), k_re=as_bf16(k_re), k_im=as_bf16(k_im),
        cc=as_bf16(np.cos(ang_c)), sc=as_bf16(np.sin(ang_c)),
        tw_cos=jnp.broadcast_to(jnp.asarray(np.cos(ang_tw), F32)[:, :, None], tw_shape),
        tw_sin=jnp.broadcast_to(jnp.asarray(np.sin(ang_tw), F32)[:, :, None], tw_shape))


def _dft1_kernel(x_ref, m1_ref, twc_ref, tws_ref, o_ref, *, n1):
    r = _dot(m1_ref[...], x_ref[...])
    twc, tws = twc_ref[0], tws_ref[0]
    for j in range(x_ref.shape[1] // LANES):
        sl = slice(j * LANES, (j + 1) * LANES)
        ar, ai = r[0:n1, sl], r[n1:, sl]
        o_ref[0, 0, :, sl] = (ar * twc + ai * tws).astype(o_ref.dtype)
        o_ref[0, 1, :, sl] = (ai * twc - ar * tws).astype(o_ref.dtype)


def _dft2_kernel(a_ref, kre_ref, kim_ref, cc_ref, sc_ref, o_ref, *, rows, n2, scale):
    a = a_ref[0].reshape(2 * rows, a_ref.shape[-1])
    zr = _dot(kre_ref[...], a).astype(BF16)
    zi = _dot(kim_ref[...], a).astype(BF16)
    cc, sc = cc_ref[...], sc_ref[...]
    for g in range(FNET_GROUPS):
        sl = slice(g * FNET_GROUP_DIM, (g + 1) * FNET_GROUP_DIM)
        y = (_dot(zr[:, sl], cc) + _dot(zi[:, sl], sc)) * scale
        o_ref[0, :, :, sl] = y.reshape(n2, DFT_INTERLEAVE, FNET_GROUP_DIM)


def fourier_real_2d(h, batch, seq_len, tabs):
    t, d = h.shape
    n1, n2 = tabs["n1"], tabs["n2"]
    rows = n2 * DFT_INTERLEAVE
    a2 = pl.pallas_call(
        functools.partial(_dft1_kernel, n1=n1),
        grid=(batch, n2),
        in_specs=[pl.BlockSpec((n1, d), lambda b, j: (b, j)),
                  _resident((2 * n1, n1), lambda b, j: (0, 0)),
                  pl.BlockSpec((1, n1, LANES), lambda b, j: (j, 0, 0)),
                  pl.BlockSpec((1, n1, LANES), lambda b, j: (j, 0, 0))],
        out_specs=pl.BlockSpec((1, 2, n1, d), lambda b, j: (b, 0, 0, j)),
        out_shape=jax.ShapeDtypeStruct((batch, 2, n1, n2 * d), BF16),
        compiler_params=_params(("parallel", "parallel")),
        name="dft_stage1",
    )(h.reshape(batch * n1, n2 * d), tabs["m1"], tabs["tw_cos"], tabs["tw_sin"])
    scale = 1.0 / math.sqrt(seq_len * FNET_GROUP_DIM)
    y = pl.pallas_call(
        functools.partial(_dft2_kernel, rows=rows, n2=n2, scale=scale),
        grid=(batch, n1 // DFT_INTERLEAVE),
        in_specs=[pl.BlockSpec((1, 2, rows, d), lambda b, j: (b, 0, j, 0)),
                  _resident((rows, 2 * rows), lambda b, j: (0, 0)),
                  _resident((rows, 2 * rows), lambda b, j: (0, 0)),
                  _resident((FNET_GROUP_DIM, FNET_GROUP_DIM), lambda b, j: (0, 0)),
                  _resident((FNET_GROUP_DIM, FNET_GROUP_DIM), lambda b, j: (0, 0))],
        out_specs=pl.BlockSpec((1, n2, DFT_INTERLEAVE, d), lambda b, j: (b, 0, j, 0)),
        out_shape=jax.ShapeDtypeStruct((batch, n2, n1, d), F32),
        compiler_params=_params(("parallel", "parallel")),
        name="dft_stage2",
    )(a2.reshape(batch, 2, n1 * n2, d), tabs["k_re"], tabs["k_im"], tabs["cc"], tabs["sc"])
    return y.reshape(t, d)


def _conf_kernel(hp_ref, hm_ref, hn_ref, w1_ref, b1_ref, dw_ref, db_ref, lg_ref, lb_ref,
                 o_ref, hh_ref, y_ref, sh_ref, *, tm, tiles_per_seq):
    d = o_ref.shape[1]
    i = pl.program_id(0)
    first, last = _seq_edges(i, tiles_per_seq)
    hh_ref[0:HALO, :] = hp_ref[...]
    hh_ref[HALO:HALO + tm, :] = hm_ref[...]
    hh_ref[HALO + tm:, :] = hn_ref[...]
    a = hh_ref[...]
    ext = tm + 2 * HALO
    rows = lax.broadcasted_iota(jnp.int32, (ext, 1), 0)
    lo = jnp.where(first, HALO, 0)
    hi = jnp.where(last, HALO + tm, ext)
    inside = (rows >= lo) & (rows < hi)
    span = tm + SUBLANES_F32 * (2 * HALO // SUBLANES_F32 - 1)
    for j in range(d // CONF_LANE_CHUNK):
        sl = slice(j * CONF_LANE_CHUNK, (j + 1) * CONF_LANE_CHUNK)
        sg = slice(d + j * CONF_LANE_CHUNK, d + (j + 1) * CONF_LANE_CHUNK)
        za = _dot(a, w1_ref[:, sl]) + b1_ref[:, sl]
        zg = _dot(a, w1_ref[:, sg]) + b1_ref[:, sg]
        u = jnp.where(inside, za * _sigmoid(zg), 0.0)
        for b in range(SUBLANES_F32):
            sh_ref[b] = u[b:b + span]
        sub = CONF_ROW_BLOCK // SUBLANES_F32
        for cc in range(CONF_LANE_CHUNK // LANES):
            lanes = slice(cc * LANES, (cc + 1) * LANES)
            col = slice(j * CONF_LANE_CHUNK + cc * LANES, j * CONF_LANE_CHUNK + (cc + 1) * LANES)
            taps = [jnp.broadcast_to(dw_ref[k:k + 1, col], (SUBLANES_F32, LANES))
                    for k in range(CONF_KERNEL)]
            bias = jnp.broadcast_to(db_ref[:, col], (SUBLANES_F32, LANES))
            for r0 in range(0, tm, CONF_ROW_BLOCK):
                acc = jnp.broadcast_to(bias[None], (sub, SUBLANES_F32, LANES))
                for k in range(CONF_KERNEL):
                    off = HALO - CONF_PAD + k
                    base = (off // SUBLANES_F32) * SUBLANES_F32 + r0
                    win = sh_ref[off % SUBLANES_F32, base:base + CONF_ROW_BLOCK, lanes]
                    acc = acc + taps[k][None] * win.reshape(sub, SUBLANES_F32, LANES)
                y_ref[r0:r0 + CONF_ROW_BLOCK, col] = acc.reshape(CONF_ROW_BLOCK, LANES)
    y = y_ref[...]
    mu = jnp.mean(y, axis=-1, keepdims=True)
    yc = y - mu
    yn = yc * lax.rsqrt(jnp.mean(yc * yc, axis=-1, keepdims=True) + EPS) * lg_ref[...] + lb_ref[...]
    o_ref[...] = (yn * _sigmoid(yn)).astype(o_ref.dtype)


def conformer_inner(h, w1, layer, b1, dw_w, dw_b, ln_g, ln_b, seq_len):
    t, d = h.shape
    tm = CONF_ROW_TILE
    n_tiles = t // tm
    fixed = lambda i: (0, 0)
    return pl.pallas_call(
        functools.partial(_conf_kernel, tm=tm, tiles_per_seq=seq_len // tm),
        grid=(n_tiles,),
        in_specs=_halo_specs(tm, d, n_tiles) + [
            _resident((None, d, 2 * d), lambda i: (layer, 0, 0)), pl.BlockSpec((1, 2 * d), fixed),
            pl.BlockSpec((CONF_KERNEL, d), fixed), pl.BlockSpec((1, d), fixed),
            pl.BlockSpec((1, d), fixed), pl.BlockSpec((1, d), fixed)],
        out_specs=pl.BlockSpec((tm, d), lambda i: (i, 0)),
        out_shape=jax.ShapeDtypeStruct((t, d), BF16),
        scratch_shapes=[pltpu.VMEM((tm + 2 * HALO, d), BF16), pltpu.VMEM((tm, d), F32),
                        pltpu.VMEM((SUBLANES_F32, tm + 2 * HALO - SUBLANES_F32, CONF_LANE_CHUNK),
                                   F32)],
        compiler_params=_params(("parallel",)),
        name="conformer_inner",
    )(h, h, h, w1, b1.reshape(1, 2 * d), dw_w, dw_b.reshape(1, d),
      ln_g.reshape(1, d), ln_b.reshape(1, d))


def _run_trunk(x, p):
    batch, seq_len, d = x.shape
    t = batch * seq_len
    xf = x.reshape(t, d)
    rope_t = _rope_tables_t(seq_len)
    tabs = _dft_tables(seq_len)
    h = rmsnorm_bf16(xf, p["norm_mix_pre"][0])
    for i in range(DEPTH):
        j, kind = i // N_MIXERS, i % N_MIXERS
        g_post, g_ffn = p["norm_mix_post"][i], p["norm_ffn_pre"][i]
        if kind == 0:
            q, k, v = qkv_project(h, p["attn_w_qkv_t"], j, p["attn_q_gain"][j],
                                  p["attn_k_gain"][j], rope_t, seq_len)
            bound = score_bound(p["attn_q_gain"][j], p["attn_k_gain"][j])
            o = flash_attention(q, k, v, bound, batch, seq_len)
            xf, h = proj_residual(o, p["attn_w_o"], j, None, xf, g_post, g_ffn)
        elif kind == 1:
            y = fourier_real_2d(h, batch, seq_len, tabs)
            xf, h = proj_residual(y, p["fnet_w_out"], j, p["fnet_b_out"][j], xf, g_post, g_ffn)
        else:
            u = conformer_inner(h, p["conv_w_pw1"], j, p["conv_b_pw1"][j], p["conv_dw_w"][j],
                                p["conv_dw_b"][j], p["conv_ln_g"][j], p["conv_ln_b"][j], seq_len)
            xf, h = proj_residual(u, p["conv_w_pw2"], j, p["conv_b_pw2"][j], xf, g_post, g_ffn)
        g_next = p["norm_mix_pre"][i + 1] if i + 1 < DEPTH else None
        xf, h = conv_ffn_residual(h, p["ffn_w_up"], p["ffn_dw_w"], p["ffn_dw_b"],
                                  p["ffn_w_down"], i, xf, p["norm_ffn_post"][i], g_next, seq_len)
    return xf.reshape(batch, seq_len, d)


def kernel(x_prompt, x_sample, norm_mix_pre, norm_mix_post, norm_ffn_pre, norm_ffn_post, attn_w_qkv, attn_q_gain, attn_k_gain, attn_w_o, fnet_w_out, fnet_b_out, conv_w_pw1, conv_b_pw1, conv_dw_w, conv_dw_b, conv_ln_g, conv_ln_b, conv_w_pw2, conv_b_pw2, ffn_w_up, ffn_dw_w, ffn_dw_b, ffn_w_down):
    p = {
        "norm_mix_pre": norm_mix_pre, "norm_mix_post": norm_mix_post,
        "norm_ffn_pre": norm_ffn_pre, "norm_ffn_post": norm_ffn_post,
        "attn_w_qkv_t": jnp.swapaxes(attn_w_qkv, 1, 2).astype(BF16),
        "attn_q_gain": attn_q_gain, "attn_k_gain": attn_k_gain,
        "attn_w_o": attn_w_o.astype(BF16),
        "fnet_w_out": fnet_w_out.astype(BF16), "fnet_b_out": fnet_b_out,
        "conv_w_pw1": conv_w_pw1.astype(BF16), "conv_b_pw1": conv_b_pw1,
        "conv_dw_w": conv_dw_w, "conv_dw_b": conv_dw_b,
        "conv_ln_g": conv_ln_g, "conv_ln_b": conv_ln_b,
        "conv_w_pw2": conv_w_pw2.astype(BF16), "conv_b_pw2": conv_b_pw2,
        "ffn_w_up": ffn_w_up.astype(BF16), "ffn_dw_w": ffn_dw_w, "ffn_dw_b": ffn_dw_b,
        "ffn_w_down": ffn_w_down.astype(BF16),
    }
    return (_run_trunk(x_prompt, p), _run_trunk(x_sample, p))
```
